```python
import jax
import jax.numpy as jnp
from jax import lax
import numpy as np

D_MODEL = 1024
BATCH = 4
SEQ = 8192
DEPTH = 1

DN_HEADS = 4
DN_HEAD_DIM = 128
DN_WIDTH = DN_HEADS * DN_HEAD_DIM
CONV_WIDTH = 4
CHUNK = 64
SWA_Q_HEADS = 8
SWA_KV_HEADS = 2
SWA_HEAD_DIM = 64
SWA_GROUP = SWA_Q_HEADS // SWA_KV_HEADS
SWA_WIDTH = SWA_Q_HEADS * SWA_HEAD_DIM
SWA_KV_WIDTH = SWA_KV_HEADS * SWA_HEAD_DIM
WINDOW = 128
ROPE_THETA = 500000.0
ROPE_DIM = SWA_HEAD_DIM // 4
N_BRANCH = 2
BRANCH_WIDTH = DN_WIDTH
DEEPNORM_ALPHA = (2.0 * DEPTH) ** 0.25
DEEPNORM_BETA = (8.0 * DEPTH) ** -0.25
LN_EPS = 1e-5
NORM_EPS = 1e-6

IN_SIZES = (DN_WIDTH, DN_WIDTH, DN_WIDTH,
            DN_WIDTH,
            DN_HEADS, DN_HEADS,
            SWA_WIDTH, SWA_KV_WIDTH, SWA_KV_WIDTH,
            SWA_WIDTH,
            D_MODEL, D_MODEL)
IN_COLS = sum(IN_SIZES)
IN_OFFSETS = tuple(int(o) for o in np.cumsum(IN_SIZES)[:-1])

kernel_name = "hybrid_gdn_swa_sink_deepnorm"


def causal_short_conv(x, w):
    return lax.conv_general_dilated(
        x, w[:, None, :].astype(x.dtype), window_strides=(1,),
        padding=[(CONV_WIDTH - 1, 0)], dimension_numbers=("NWC", "WIO", "NWC"),
        feature_group_count=x.shape[-1])


def l2norm(t):
    t = t.astype(jnp.float32)
    return t * lax.rsqrt(jnp.sum(t * t, axis=-1, keepdims=True) + NORM_EPS)


def chunked_gated_delta_rule(q, k, v, g, beta):
    b, s, h, dk = q.shape
    dv = v.shape[-1]
    n = s // CHUNK

    def chunks(t):
        return t.reshape(b, n, CHUNK, h, -1).transpose(0, 1, 3, 2, 4)

    q, k, v = chunks(q), chunks(k), chunks(v)
    g = g.reshape(b, n, CHUNK, h).transpose(0, 1, 3, 2)
    beta = beta.reshape(b, n, CHUNK, h).transpose(0, 1, 3, 2)
    g = jnp.cumsum(g, axis=-1)

    pos = jnp.arange(CHUNK)
    causal = pos[:, None] >= pos[None, :]
    strict = pos[:, None] > pos[None, :]
    decay = jnp.where(causal, jnp.exp(jnp.where(causal, g[..., :, None] - g[..., None, :], 0.0)), 0.0)

    k_beta = k * beta[..., None]
    lower = jnp.where(strict, jnp.einsum("bnhid,bnhjd->bnhij", k_beta, k) * decay, 0.0)
    eye = jnp.eye(CHUNK, dtype=jnp.float32)
    t_inv = lax.linalg.triangular_solve(eye + lower, jnp.broadcast_to(eye, lower.shape),
                                        left_side=True, lower=True, unit_diagonal=True)
    u = jnp.einsum("bnhij,bnhjd->bnhid", t_inv, v * beta[..., None])
    w = jnp.einsum("bnhij,bnhjd->bnhid", t_inv, k_beta * jnp.exp(g)[..., None])
    intra = jnp.where(causal, jnp.einsum("bnhid,bnhjd->bnhij", q, k) * decay, 0.0)
    q_decayed = q * jnp.exp(g)[..., None]
    g_last = g[..., -1]
    k_tail = k * jnp.exp(g_last[..., None] - g)[..., None]

    def step(state, xs):
        qd_i, a_i, u_i, w_i, kt_i, gl_i = xs
        v_new = u_i - jnp.einsum("bhcd,bhde->bhce", w_i, state)
        o_i = jnp.einsum("bhcd,bhde->bhce", qd_i, state) + jnp.einsum("bhij,bhje->bhie", a_i, v_new)
        state = state * jnp.exp(gl_i)[..., None, None] + jnp.einsum("bhcd,bhce->bhde", kt_i, v_new)
        return state, o_i

    xs = tuple(jnp.moveaxis(t, 1, 0) for t in (q_decayed, intra, u, w, k_tail, g_last))
    _, out = lax.scan(step, jnp.zeros((b, h, dk, dv), jnp.float32), xs)
    return out.transpose(1, 0, 3, 2, 4).reshape(b, s, h, dv)


def rope_tables(seq):
    inv_freq = ROPE_THETA ** (-jnp.arange(0, ROPE_DIM, 2, dtype=jnp.float32) / ROPE_DIM)
    ang = jnp.arange(seq, dtype=jnp.float32)[:, None] * inv_freq[None, :]
    return jnp.cos(ang), jnp.sin(ang)


def partial_rope(t, cos, sin):
    half = ROPE_DIM // 2
    c = cos[None, :, None, :].astype(t.dtype)
    s = sin[None, :, None, :].astype(t.dtype)
    t1 = t[..., :half]
    t2 = t[..., half:ROPE_DIM]
    return jnp.concatenate([t1 * c - t2 * s, t2 * c + t1 * s, t[..., ROPE_DIM:]], axis=-1)


def banded_sink_attention(q, k, v, sinks):
    b, s, _, d = q.shape
    nb = s // WINDOW
    qb = q.reshape(b, nb, WINDOW, SWA_KV_HEADS, SWA_GROUP, d)

    def band(t):
        tb = t.reshape(b, nb, WINDOW, SWA_KV_HEADS, d)
        prev = jnp.pad(tb, ((0, 0), (1, 0), (0, 0), (0, 0), (0, 0)))[:, :-1]
        return jnp.concatenate([prev, tb], axis=2)

    kb, vb = band(k), band(v)
    scores = jnp.einsum("bnqhgd,bnkhd->bnhgqk", qb, kb).astype(jnp.float32) * (d ** -0.5)
    qpos = jnp.arange(WINDOW)[:, None] + WINDOW
    kpos = jnp.arange(2 * WINDOW)[None, :]
    dist = qpos - kpos
    in_band = (dist >= 0) & (dist < WINDOW)
    blk = jnp.arange(nb)[:, None, None]
    valid = in_band[None] & ((blk > 0) | (kpos >= WINDOW)[None])
    scores = jnp.where(valid[None, :, None, None], scores, -jnp.inf)
    sink = sinks.astype(jnp.float32).reshape(SWA_KV_HEADS, SWA_GROUP)[None, None, :, :, None, None]
    m = jnp.maximum(jnp.max(scores, axis=-1, keepdims=True), sink)
    p = jnp.exp(scores - m)
    probs = p / (jnp.sum(p, axis=-1, keepdims=True) + jnp.exp(sink - m))
    out = jnp.einsum("bnhgqk,bnkhd->bnqhgd", probs.astype(v.dtype), vb)
    return out.reshape(b, s, SWA_Q_HEADS * d)


def hybrid_layer(x, w_in, conv_w, a_log, dt_bias, dn_norm_w, sinks, w_branch, w_out, ln_g, ln_b, cos, sin):
    b, s, _ = x.shape
    h = jnp.einsum("bsd,dc->bsc", x, w_in)
    (_, _, _, dn_z, dn_b, dn_a, sw_q, sw_k, sw_v, sw_z, gate_a, gate_b) = jnp.split(h, IN_OFFSETS, axis=-1)

    qkv = jax.nn.silu(causal_short_conv(h[..., :3 * DN_WIDTH], conv_w))
    dq, dk, dv = jnp.split(qkv, 3, axis=-1)
    q_a = l2norm(dq.reshape(b, s, DN_HEADS, DN_HEAD_DIM)) * (DN_HEAD_DIM ** -0.5)
    k_a = l2norm(dk.reshape(b, s, DN_HEADS, DN_HEAD_DIM))
    v_a = dv.reshape(b, s, DN_HEADS, DN_HEAD_DIM).astype(jnp.float32)
    beta = jax.nn.sigmoid(dn_b.astype(jnp.float32))
    g = -jnp.exp(a_log.astype(jnp.float32)) * jax.nn.softplus(dn_a.astype(jnp.float32) + dt_bias.astype(jnp.float32))
    o_a = chunked_gated_delta_rule(q_a, k_a, v_a, g, beta)
    o_a = o_a * lax.rsqrt(jnp.mean(o_a * o_a, axis=-1, keepdims=True) + NORM_EPS) * dn_norm_w.astype(jnp.float32)
    y_a = (o_a.astype(x.dtype) * jax.nn.silu(dn_z.reshape(b, s, DN_HEADS, DN_HEAD_DIM))).reshape(b, s, DN_WIDTH)

    q_b = partial_rope(sw_q.reshape(b, s, SWA_Q_HEADS, SWA_HEAD_DIM), cos, sin)
    k_b = partial_rope(sw_k.reshape(b, s, SWA_KV_HEADS, SWA_HEAD_DIM), cos, sin)
    v_b = sw_v.reshape(b, s, SWA_KV_HEADS, SWA_HEAD_DIM)
    y_b = banded_sink_attention(q_b, k_b, v_b, sinks) * jax.nn.silu(sw_z)

    merged = (jax.nn.sigmoid(gate_a) * jnp.einsum("bsc,cd->bsd", y_a, w_branch[0])
              + jax.nn.sigmoid(gate_b) * jnp.einsum("bsc,cd->bsd", y_b, w_branch[1]))
    out = jnp.einsum("bsd,de->bse", merged, w_out)

    r = (DEEPNORM_ALPHA * x + out).astype(jnp.float32)
    mu = jnp.mean(r, axis=-1, keepdims=True)
    var = jnp.mean(jnp.square(r - mu), axis=-1, keepdims=True)
    y = (r - mu) * lax.rsqrt(var + LN_EPS) * ln_g.astype(jnp.float32) + ln_b.astype(jnp.float32)
    return y.astype(x.dtype)


def setup_inputs(seed: int = 0) -> dict:
    key = jax.random.key(seed)
    ks = jax.random.split(key, 11)
    x = jax.random.normal(ks[0], (BATCH, SEQ, D_MODEL), jnp.float32)
    seg_scale = (1.0, 1.0, DEEPNORM_BETA, 1.0, 1.0, 1.0, 1.0, 1.0, DEEPNORM_BETA, 1.0, 1.0, 1.0)
    col_scale = np.concatenate([np.full((n,), sc, np.float32) for n, sc in zip(IN_SIZES, seg_scale)])
    w_in = (jax.random.normal(ks[1], (DEPTH, D_MODEL, IN_COLS), jnp.float32)
            * (D_MODEL ** -0.5) * jnp.asarray(col_scale, jnp.float32))
    conv_w = jax.random.normal(ks[2], (DEPTH, CONV_WIDTH, 3 * DN_WIDTH), jnp.float32) * (CONV_WIDTH ** -0.5)
    a_log = jnp.log(jax.random.uniform(ks[3], (DEPTH, DN_HEADS), jnp.float32, minval=1.0, maxval=16.0))
    dt = jnp.exp(jax.random.uniform(ks[4], (DEPTH, DN_HEADS), jnp.float32,
                                    minval=float(np.log(1e-3)), maxval=float(np.log(1e-1))))
    dt_bias = dt + jnp.log(-jnp.expm1(-dt))
    dn_norm_w = 1.0 + 0.02 * jax.random.normal(ks[5], (DEPTH, DN_HEAD_DIM), jnp.float32)
    sinks = 0.5 * jax.random.normal(ks[6], (DEPTH, SWA_Q_HEADS), jnp.float32)
    w_branch = (jax.random.normal(ks[7], (DEPTH, N_BRANCH, BRANCH_WIDTH, D_MODEL), jnp.float32)
                * (BRANCH_WIDTH ** -0.5) * DEEPNORM_BETA)
    w_out = jax.random.normal(ks[8], (DEPTH, D_MODEL, D_MODEL), jnp.float32) * (D_MODEL ** -0.5) * DEEPNORM_BETA
    ln_g = 1.0 + 0.02 * jax.random.normal(ks[9], (DEPTH, D_MODEL), jnp.float32)
    ln_b = 0.02 * jax.random.normal(ks[10], (DEPTH, D_MODEL), jnp.float32)
    return {"x": x, "w_in": w_in, "conv_w": conv_w, "a_log": a_log, "dt_bias": dt_bias,
            "dn_norm_w": dn_norm_w, "sinks": sinks, "w_branch": w_branch, "w_out": w_out,
            "ln_g": ln_g, "ln_b": ln_b}


def reference(x, w_in, conv_w, a_log, dt_bias, dn_norm_w, sinks, w_branch, w_out, ln_g, ln_b):
    cos, sin = rope_tables(x.shape[1])
    for layer in range(DEPTH):
        x = hybrid_layer(x, w_in[layer], conv_w[layer], a_log[layer], dt_bias[layer], dn_norm_w[layer],
                         sinks[layer], w_branch[layer], w_out[layer], ln_g[layer], ln_b[layer], cos, sin)
    return x
```

```python
import functools

import jax
import jax.numpy as jnp
from jax import lax
from jax.experimental import pallas as pl
from jax.experimental.pallas import tpu as pltpu

F32 = jnp.float32
BF16 = jnp.bfloat16

D_MODEL = 1024
DN_HEADS = 4
DN_HEAD_DIM = 128
DN_WIDTH = DN_HEADS * DN_HEAD_DIM
CONV_WIDTH = 4
CHUNK = 64
SWA_Q_HEADS = 8
SWA_KV_HEADS = 2
SWA_HEAD_DIM = 64
SWA_WIDTH = SWA_Q_HEADS * SWA_HEAD_DIM
SWA_KV_WIDTH = SWA_KV_HEADS * SWA_HEAD_DIM
WINDOW = 128
ROPE_THETA = 500000.0
ROPE_DIM = SWA_HEAD_DIM // 4
LN_EPS = 1e-5
NORM_EPS = 1e-6
MASK_VALUE = -1e30

LANES = 128
SUBLANES = 8
VMEM_LIMIT_BYTES = 48 * 1024 * 1024

DN_TILE = 256
MERGE_TILE = 256

_OFF_QKV = 0
_OFF_DN_Z = 3 * DN_WIDTH
_OFF_DN_BA = _OFF_DN_Z + DN_WIDTH
_OFF_SWA = _OFF_DN_BA + 2 * DN_HEADS
_SWA_COLS = SWA_WIDTH + 2 * SWA_KV_WIDTH + SWA_WIDTH
_OFF_GATE = _OFF_SWA + _SWA_COLS


def _mm(a, b):
    return jnp.dot(a.astype(BF16), b.astype(BF16), preferred_element_type=F32)


def _mm_nt(a, b):
    return lax.dot_general(a.astype(BF16), b.astype(BF16), (((1,), (1,)), ((), ())),
                           preferred_element_type=F32)


def _mm_tn(a, b):
    return lax.dot_general(a.astype(BF16), b.astype(BF16), (((0,), (0,)), ((), ())),
                           preferred_element_type=F32)


def _silu(v):
    return v * jax.nn.sigmoid(v)


def _unit_lower_inverse(low, xor_ij, eye):
    inv = eye - jnp.where(xor_ij == 1, low, 0.0)
    level = 1
    while (1 << level) < CHUNK:
        c = jnp.where((xor_ij >> level) == 1, low, 0.0)
        inv = inv - _mm(_mm(inv, c), inv)
        level += 1
    return inv


def _dn_kernel(x_ref, wqkv_ref, wz_ref, wba_ref, convw_ref, alog_ref, dtb_ref, normw_ref, y_ref,
               hbuf, q_s, k_s, v_s, z_s, state):
    t = pl.program_id(1)
    tile = x_ref.shape[0]
    halo = SUBLANES

    @pl.when(t == 0)
    def _():
        hbuf[0:halo, :] = jnp.zeros((halo, 3 * DN_WIDTH), F32)
        state[...] = jnp.zeros_like(state)

    xb = x_ref[...].astype(BF16)
    hbuf[halo:halo + tile, :] = jnp.dot(xb, wqkv_ref[...], preferred_element_type=F32)
    z_s[...] = jnp.dot(xb, wz_ref[...], preferred_element_type=F32)
    ba = jnp.dot(xb, wba_ref[...], preferred_element_type=F32)

    dests = (q_s, k_s, v_s)
    for s in range(3 * DN_HEADS):
        c0 = s * DN_HEAD_DIM
        acc = None
        for j in range(CONV_WIDTH):
            r0 = halo - (CONV_WIDTH - 1) + j
            term = convw_ref[j:j + 1, c0:c0 + DN_HEAD_DIM] * hbuf[r0:r0 + tile, c0:c0 + DN_HEAD_DIM]
            acc = term if acc is None else acc + term
        a = _silu(acc)
        if s < 2 * DN_HEADS:
            a = a * lax.rsqrt(jnp.sum(a * a, axis=-1, keepdims=True) + NORM_EPS)
            if s < DN_HEADS:
                a = a * (DN_HEAD_DIM ** -0.5)
        h0 = (s % DN_HEADS) * DN_HEAD_DIM
        dests[s // DN_HEADS][:, h0:h0 + DN_HEAD_DIM] = a
    hbuf[0:halo, :] = hbuf[tile:tile + halo, :]

    beta = jax.nn.sigmoid(ba)
    xg = ba + dtb_ref[...]
    softplus = jnp.maximum(xg, 0.0) + jnp.log1p(jnp.exp(-jnp.abs(xg)))
    g = -jnp.exp(alog_ref[...]) * softplus

    pos = lax.broadcasted_iota(jnp.int32, (tile, LANES), 0) % CHUNK
    gc = g
    step = 1
    while step < CHUNK:
        gc = gc + jnp.where(pos >= step, pltpu.roll(gc, step, 0), 0.0)
        step *= 2
    gc_t = gc.T

    row = lax.broadcasted_iota(jnp.int32, (CHUNK, CHUNK), 0)
    col = lax.broadcasted_iota(jnp.int32, (CHUNK, CHUNK), 1)
    causal = row >= col
    strict = row > col
    xor_ij = row ^ col
    eye = jnp.where(row == col, 1.0, 0.0).astype(F32)

    for c in range(tile // CHUNK):
        r0 = c * CHUNK
        gc_c = gc[r0:r0 + CHUNK, :]
        g_last = gc[r0 + CHUNK - 1:r0 + CHUNK, :]
        e_gc = jnp.exp(gc_c)
        e_tail = jnp.exp(g_last - gc_c)
        e_last = jnp.exp(g_last)
        beta_c = beta[r0:r0 + CHUNK, :]
        for h in range(DN_HEADS):
            h0 = h * DN_HEAD_DIM
            la = DN_HEADS + h
            qh = q_s[r0:r0 + CHUNK, h0:h0 + DN_HEAD_DIM]
            kh = k_s[r0:r0 + CHUNK, h0:h0 + DN_HEAD_DIM]
            vh = v_s[r0:r0 + CHUNK, h0:h0 + DN_HEAD_DIM]
            g_col = gc_c[:, la:la + 1]
            g_row = gc_t[la:la + 1, r0:r0 + CHUNK]
            b_col = beta_c[:, h:h + 1]
            eg_col = e_gc[:, la:la + 1]
            decay = jnp.where(causal, jnp.exp(jnp.where(causal, g_col - g_row, 0.0)), 0.0)

            kb = kh * b_col
            kq = _mm_nt(jnp.concatenate([kb, qh], axis=0), kh)
            low = jnp.where(strict, kq[:CHUNK] * decay, 0.0)
            intra = jnp.where(causal, kq[CHUNK:] * decay, 0.0)
            t_inv = _unit_lower_inverse(low, xor_ij, eye)
            uw = _mm(t_inv, jnp.concatenate([vh * b_col, kb * eg_col], axis=1))
            u = uw[:, :DN_HEAD_DIM]
            w = uw[:, DN_HEAD_DIM:]

            s_h = state[h]
            ws_qs = _mm(jnp.concatenate([w, qh * eg_col], axis=0), s_h)
            v_new = u - ws_qs[:CHUNK]
            o = ws_qs[CHUNK:] + _mm(intra, v_new)
            k_tail = kh * e_tail[:, la:la + 1]
            state[h] = s_h * e_last[:, la:la + 1] + _mm_tn(k_tail, v_new)

            o = o * lax.rsqrt(jnp.mean(o * o, axis=-1, keepdims=True) + NORM_EPS) * normw_ref[...]
            zc = z_s[r0:r0 + CHUNK, h0:h0 + DN_HEAD_DIM]
            y_ref[r0:r0 + CHUNK, h0:h0 + DN_HEAD_DIM] = (o * _silu(zc)).astype(y_ref.dtype)


def _deltanet_branch(x, wqkv, wz, wba, conv_w, alog_row, dtb_row, normw_row):
    b, s, d = x.shape
    tile = DN_TILE
    const = lambda bi, ti: (0, 0)
    return pl.pallas_call(
        _dn_kernel,
        grid=(b, s // tile),
        in_specs=[
            pl.BlockSpec((None, tile, d), lambda bi, ti: (bi, ti, 0)),
            pl.BlockSpec(wqkv.shape, const),
            pl.BlockSpec(wz.shape, const),
            pl.BlockSpec(wba.shape, const),
            pl.BlockSpec(conv_w.shape, const),
            pl.BlockSpec(alog_row.shape, const),
            pl.BlockSpec(dtb_row.shape, const),
            pl.BlockSpec(normw_row.shape, const),
        ],
        out_specs=pl.BlockSpec((None, tile, DN_WIDTH), lambda bi, ti: (bi, ti, 0)),
        out_shape=jax.ShapeDtypeStruct((b, s, DN_WIDTH), BF16),
        scratch_shapes=[
            pltpu.VMEM((tile + SUBLANES, 3 * DN_WIDTH), F32),
            pltpu.VMEM((tile, DN_WIDTH), F32),
            pltpu.VMEM((tile, DN_WIDTH), F32),
            pltpu.VMEM((tile, DN_WIDTH), F32),
            pltpu.VMEM((tile, DN_WIDTH), F32),
            pltpu.VMEM((DN_HEADS, DN_HEAD_DIM, DN_HEAD_DIM), F32),
        ],
        compiler_params=pltpu.CompilerParams(
            dimension_semantics=("arbitrary", "arbitrary"), vmem_limit_bytes=VMEM_LIMIT_BYTES),
        name="deltanet_branch",
    )(x, wqkv, wz, wba, conv_w, alog_row, dtb_row, normw_row)


def _swa_kernel(sinks_ref, x_ref, wq_ref, wkv_ref, wz_ref, rope_ref, y_ref, kband, vband):
    t = pl.program_id(1)
    w = WINDOW

    @pl.when(t == 0)
    def _():
        kband[:, 0:w, :] = jnp.zeros((2 * SWA_KV_HEADS, w, LANES), BF16)
        vband[:, 0:w, :] = jnp.zeros((2 * SWA_KV_HEADS, w, LANES), BF16)

    cos_p = rope_ref[:, 0:LANES]
    sin_a = rope_ref[:, LANES:2 * LANES]
    sin_b = rope_ref[:, 2 * LANES:3 * LANES]
    half = ROPE_DIM // 2

    def rope(v):
        return v * cos_p + pltpu.roll(v, LANES - half, 1) * sin_a + pltpu.roll(v, half, 1) * sin_b

    xb = x_ref[...].astype(BF16)
    q = jnp.dot(xb, wq_ref[...], preferred_element_type=F32)
    kv = jnp.dot(xb, wkv_ref[...], preferred_element_type=F32)
    z = jnp.dot(xb, wz_ref[...], preferred_element_type=F32)

    k = rope(kv[:, :LANES])
    v = kv[:, LANES:]
    lane = lax.broadcasted_iota(jnp.int32, (w, LANES), 1)
    lo = lane < SWA_HEAD_DIM
    for src, band in ((k, kband), (v, vband)):
        swapped = pltpu.roll(src, SWA_HEAD_DIM, 1)
        band[0, w:2 * w, :] = jnp.where(lo, src, 0.0).astype(BF16)
        band[1, w:2 * w, :] = jnp.where(lo, 0.0, swapped).astype(BF16)
        band[2, w:2 * w, :] = jnp.where(lo, swapped, 0.0).astype(BF16)
        band[3, w:2 * w, :] = jnp.where(lo, 0.0, src).astype(BF16)

    qi = lax.broadcasted_iota(jnp.int32, (w, 2 * w), 0)
    kj = lax.broadcasted_iota(jnp.int32, (w, 2 * w), 1)
    first_valid = jnp.where(t == 0, w, 0)
    valid = (kj > qi) & (kj <= qi + w) & (kj >= first_valid)
    bias = jnp.where(valid, 0.0, MASK_VALUE).astype(F32)

    group = SWA_Q_HEADS // SWA_KV_HEADS
    for pair in range(SWA_Q_HEADS // 2):
        p0 = pair * LANES
        qp = rope(q[:, p0:p0 + LANES]).astype(BF16)
        acc = None
        for hf in range(2):
            head = 2 * pair + hf
            idx = 2 * (head // group) + hf
            sink = sinks_ref[head]
            s = _mm_nt(qp, kband[idx]) + bias
            m = jnp.maximum(jnp.max(s, axis=-1, keepdims=True), sink)
            p = jnp.exp(s - m)
            denom = jnp.sum(p, axis=-1, keepdims=True) + jnp.exp(sink - m)
            o = _mm(p, vband[idx]) / denom
            acc = o if acc is None else acc + o
        y_ref[:, p0:p0 + LANES] = (acc * _silu(z[:, p0:p0 + LANES])).astype(y_ref.dtype)

    kband[:, 0:w, :] = kband[:, w:2 * w, :]
    vband[:, 0:w, :] = vband[:, w:2 * w, :]


def _swa_branch(x, sinks, wq, wkv, wz, rope_tab):
    b, s, d = x.shape
    w = WINDOW
    const = lambda bi, ti: (0, 0)
    return pl.pallas_call(
        _swa_kernel,
        grid=(b, s // w),
        in_specs=[
            pl.BlockSpec(memory_space=pltpu.SMEM),
            pl.BlockSpec((None, w, d), lambda bi, ti: (bi, ti, 0)),
            pl.BlockSpec(wq.shape, const),
            pl.BlockSpec(wkv.shape, const),
            pl.BlockSpec(wz.shape, const),
            pl.BlockSpec((w, 3 * LANES), lambda bi, ti: (ti, 0)),
        ],
        out_specs=pl.BlockSpec((None, w, SWA_WIDTH), lambda bi, ti: (bi, ti, 0)),
        out_shape=jax.ShapeDtypeStruct((b, s, SWA_WIDTH), BF16),
        scratch_shapes=[
            pltpu.VMEM((2 * SWA_KV_HEADS, 2 * w, LANES), BF16),
            pltpu.VMEM((2 * SWA_KV_HEADS, 2 * w, LANES), BF16),
        ],
        compiler_params=pltpu.CompilerParams(
            dimension_semantics=("arbitrary", "arbitrary"), vmem_limit_bytes=VMEM_LIMIT_BYTES),
        name="swa_branch",
    )(sinks, x, wq, wkv, wz, rope_tab)


def _rope_table(seq):
    half = ROPE_DIM // 2
    inv_freq = ROPE_THETA ** (-jnp.arange(0, ROPE_DIM, 2, dtype=F32) / ROPE_DIM)
    ang = jnp.arange(seq, dtype=F32)[:, None] * inv_freq[None, :]
    cos, sin = jnp.cos(ang), jnp.sin(ang)
    rest = SWA_HEAD_DIM - ROPE_DIM
    ones = jnp.ones((seq, rest), F32)
    zeros = jnp.zeros((seq, rest), F32)
    zh = jnp.zeros((seq, half), F32)
    cos_p = jnp.concatenate([cos, cos, ones], axis=1)
    sin_a = jnp.concatenate([-sin, zh, zeros], axis=1)
    sin_b = jnp.concatenate([zh, sin, zeros], axis=1)
    reps = LANES // SWA_HEAD_DIM
    return jnp.concatenate([jnp.tile(cos_p, (1, reps)), jnp.tile(sin_a, (1, reps)),
                            jnp.tile(sin_b, (1, reps))], axis=1)


def _merge_kernel(alpha, x_ref, ya_ref, yb_ref, wg_ref, wa_ref, wb_ref, wo_ref, lng_ref, lnb_ref, o_ref):
    x = x_ref[...]
    gates = jnp.dot(x.astype(BF16), wg_ref[...], preferred_element_type=F32)
    pa = jnp.dot(ya_ref[...], wa_ref[...], preferred_element_type=F32)
    pb = jnp.dot(yb_ref[...], wb_ref[...], preferred_element_type=F32)
    merged = (jax.nn.sigmoid(gates[:, :D_MODEL]) * pa + jax.nn.sigmoid(gates[:, D_MODEL:]) * pb)
    out = jnp.dot(merged.astype(BF16), wo_ref[...], preferred_element_type=F32)
    r = alpha * x + out
    mu = jnp.mean(r, axis=-1, keepdims=True)
    cen = r - mu
    var = jnp.mean(cen * cen, axis=-1, keepdims=True)
    o_ref[...] = (cen * lax.rsqrt(var + LN_EPS) * lng_ref[...] + lnb_ref[...]).astype(o_ref.dtype)


def _merge(x2, ya2, yb2, wg, wa, wb, wo, lng_row, lnb_row, alpha):
    n, d = x2.shape
    tile = MERGE_TILE
    const = lambda i: (0, 0)
    row = lambda i: (i, 0)
    return pl.pallas_call(
        functools.partial(_merge_kernel, alpha),
        grid=(n // tile,),
        in_specs=[
            pl.BlockSpec((tile, d), row),
            pl.BlockSpec((tile, ya2.shape[1]), row),
            pl.BlockSpec((tile, yb2.shape[1]), row),
            pl.BlockSpec(wg.shape, const),
            pl.BlockSpec(wa.shape, const),
            pl.BlockSpec(wb.shape, const),
            pl.BlockSpec(wo.shape, const),
            pl.BlockSpec(lng_row.shape, const),
            pl.BlockSpec(lnb_row.shape, const),
        ],
        out_specs=pl.BlockSpec((tile, d), row),
        out_shape=jax.ShapeDtypeStruct((n, d), x2.dtype),
        compiler_params=pltpu.CompilerParams(
            dimension_semantics=("arbitrary",), vmem_limit_bytes=VMEM_LIMIT_BYTES),
        name="merge_out_norm",
    )(x2, ya2, yb2, wg, wa, wb, wo, lng_row, lnb_row)


def _lane_row(vec, offset):
    n = vec.shape[0]
    return jnp.pad(vec.astype(F32), (offset, LANES - offset - n)).reshape(1, LANES)


def _layer(x, w_in, conv_w, a_log, dt_bias, dn_norm_w, sinks, w_branch, w_out, ln_g, ln_b, rope_tab, alpha):
    b, s, d = x.shape
    wqkv = w_in[:, _OFF_QKV:_OFF_DN_Z].astype(BF16)
    wz_a = w_in[:, _OFF_DN_Z:_OFF_DN_BA].astype(BF16)
    wba = jnp.pad(w_in[:, _OFF_DN_BA:_OFF_SWA], ((0, 0), (0, LANES - 2 * DN_HEADS))).astype(BF16)
    y_a = _deltanet_branch(x, wqkv, wz_a, wba, conv_w.astype(F32), _lane_row(a_log, DN_HEADS),
                           _lane_row(dt_bias, DN_HEADS), dn_norm_w.astype(F32).reshape(1, DN_HEAD_DIM))

    o = _OFF_SWA
    wq = (w_in[:, o:o + SWA_WIDTH] * (SWA_HEAD_DIM ** -0.5)).astype(BF16)
    wkv = w_in[:, o + SWA_WIDTH:o + SWA_WIDTH + 2 * SWA_KV_WIDTH].astype(BF16)
    wz_b = w_in[:, o + SWA_WIDTH + 2 * SWA_KV_WIDTH:_OFF_GATE].astype(BF16)
    y_b = _swa_branch(x, sinks.astype(F32), wq, wkv, wz_b, rope_tab)

    wg = w_in[:, _OFF_GATE:].astype(BF16)
    out = _merge(x.reshape(b * s, d), y_a.reshape(b * s, DN_WIDTH), y_b.reshape(b * s, SWA_WIDTH),
                 wg, w_branch[0].astype(BF16), w_branch[1].astype(BF16), w_out.astype(BF16),
                 ln_g.astype(F32).reshape(1, d), ln_b.astype(F32).reshape(1, d), alpha)
    return out.reshape(b, s, d)


def kernel(x, w_in, conv_w, a_log, dt_bias, dn_norm_w, sinks, w_branch, w_out, ln_g, ln_b):
    depth = w_in.shape[0]
    alpha = (2.0 * depth) ** 0.25
    rope_tab = _rope_table(x.shape[1])
    for layer in range(depth):
        x = _layer(x, w_in[layer], conv_w[layer], a_log[layer], dt_bias[layer], dn_norm_w[layer],
                   sinks[layer], w_branch[layer], w_out[layer], ln_g[layer], ln_b[layer], rope_tab, alpha)
    return x
```

```python
import functools

import jax
import jax.numpy as jnp
from jax import lax
from jax.experimental import pallas as pl
from jax.experimental.pallas import tpu as pltpu

F32 = jnp.float32
BF16 = jnp.bfloat16

D_MODEL = 1024
DN_HEADS = 4
DN_HEAD_DIM = 128
DN_WIDTH = DN_HEADS * DN_HEAD_DIM
CONV_WIDTH = 4
CHUNK = 64
SWA_Q_HEADS = 8
SWA_KV_HEADS = 2
SWA_HEAD_DIM = 64
SWA_WIDTH = SWA_Q_HEADS * SWA_HEAD_DIM
SWA_KV_WIDTH = SWA_KV_HEADS * SWA_HEAD_DIM
WINDOW = 128
ROPE_THETA = 500000.0
ROPE_DIM = SWA_HEAD_DIM // 4
LN_EPS = 1e-5
NORM_EPS = 1e-6
MASK_VALUE = -1e30

LANES = 128
SUBLANES = 8
VMEM_LIMIT_BYTES = 48 * 1024 * 1024

DN_TILE = 256
MERGE_TILE = 256

_OFF_QKV = 0
_OFF_DN_Z = 3 * DN_WIDTH
_OFF_DN_BA = _OFF_DN_Z + DN_WIDTH
_OFF_SWA = _OFF_DN_BA + 2 * DN_HEADS
_SWA_COLS = SWA_WIDTH + 2 * SWA_KV_WIDTH + SWA_WIDTH
_OFF_GATE = _OFF_SWA + _SWA_COLS


def _mm(a, b):
    return jnp.dot(a.astype(BF16), b.astype(BF16), preferred_element_type=F32)


def _mm_nt(a, b):
    return lax.dot_general(a.astype(BF16), b.astype(BF16), (((1,), (1,)), ((), ())),
                           preferred_element_type=F32)


def _mm_tn(a, b):
    return lax.dot_general(a.astype(BF16), b.astype(BF16), (((0,), (0,)), ((), ())),
                           preferred_element_type=F32)


def _silu(v):
    return v * jax.nn.sigmoid(v)


def _unit_lower_inverses(lows, xor_ij, eye):
    invs = [eye - jnp.where(xor_ij == 1, low, 0.0) for low in lows]
    level = 1
    while (1 << level) < CHUNK:
        joins = (xor_ij >> level) == 1
        cs = [jnp.where(joins, low, 0.0).astype(BF16) for low in lows]
        xs = [inv.astype(BF16) for inv in invs]
        xcs = [_mm(x, c) for x, c in zip(xs, cs)]
        invs = [inv - _mm(xc, x) for inv, xc, x in zip(invs, xcs, xs)]
        level += 1
    return invs


def _dn_kernel(x_ref, wqkv_ref, wz_ref, wba_ref, convw_ref, alog_ref, dtb_ref, normw_ref, y_ref,
               hbuf, q_s, k_s, v_s, z_s, u_s, w_s, qd_s, kt_s, a_s, state):
    t = pl.program_id(1)
    tile = x_ref.shape[0]
    halo = SUBLANES
    pair = 2 * CHUNK
    n_pairs = tile // pair
    heads = range(DN_HEADS)

    @pl.when(t == 0)
    def _():
        hbuf[0:halo, :] = jnp.zeros((halo, 3 * DN_WIDTH), F32)
        state[...] = jnp.zeros_like(state)

    xb = x_ref[...].astype(BF16)
    hbuf[halo:halo + tile, :] = jnp.dot(xb, wqkv_ref[...], preferred_element_type=F32)
    z_s[...] = jnp.dot(xb, wz_ref[...], preferred_element_type=F32)
    ba = jnp.dot(xb, wba_ref[...], preferred_element_type=F32)

    dests = (q_s, k_s, v_s)
    for s in range(3 * DN_HEADS):
        c0 = s * DN_HEAD_DIM
        acc = None
        for j in range(CONV_WIDTH):
            r0 = halo - (CONV_WIDTH - 1) + j
            term = convw_ref[j:j + 1, c0:c0 + DN_HEAD_DIM] * hbuf[r0:r0 + tile, c0:c0 + DN_HEAD_DIM]
            acc = term if acc is None else acc + term
        a = _silu(acc)
        if s < 2 * DN_HEADS:
            a = a * lax.rsqrt(jnp.sum(a * a, axis=-1, keepdims=True) + NORM_EPS)
            if s < DN_HEADS:
                a = a * (DN_HEAD_DIM ** -0.5)
        h0 = (s % DN_HEADS) * DN_HEAD_DIM
        dests[s // DN_HEADS][:, h0:h0 + DN_HEAD_DIM] = a
    hbuf[0:halo, :] = hbuf[tile:tile + halo, :]

    beta = jax.nn.sigmoid(ba)
    xg = ba + dtb_ref[...]
    softplus = jnp.maximum(xg, 0.0) + jnp.log1p(jnp.exp(-jnp.abs(xg)))
    g = -jnp.exp(alog_ref[...]) * softplus

    pos = lax.broadcasted_iota(jnp.int32, (tile, LANES), 0) % CHUNK
    gc = g
    step = 1
    while step < CHUNK:
        gc = gc + jnp.where(pos >= step, pltpu.roll(gc, step, 0), 0.0)
        step *= 2
    gc_t = gc.T

    row = lax.broadcasted_iota(jnp.int32, (pair, pair), 0)
    col = lax.broadcasted_iota(jnp.int32, (pair, pair), 1)
    same_chunk = (row // CHUNK) == (col // CHUNK)
    causal = same_chunk & (row >= col)
    strict = same_chunk & (row > col)
    xor_ij = row ^ col
    eye = jnp.where(row == col, 1.0, 0.0).astype(F32)
    first_chunk = lax.broadcasted_iota(jnp.int32, (pair, LANES), 0) < CHUNK

    items = [(p, h) for p in range(n_pairs) for h in heads]
    e_last = []
    lows, rhs = [], []
    for p in range(n_pairs):
        r0 = p * pair
        gc_p = gc[r0:r0 + pair, :]
        last0 = gc_p[CHUNK - 1:CHUNK, :]
        last1 = gc_p[pair - 1:pair, :]
        e_gc = jnp.exp(gc_p)
        e_tail = jnp.exp(jnp.where(first_chunk, last0, last1) - gc_p)
        e_last.append((jnp.exp(last0), jnp.exp(last1)))
        beta_p = beta[r0:r0 + pair, :]
        for h in heads:
            h0 = h * DN_HEAD_DIM
            la = DN_HEADS + h
            qh = q_s[r0:r0 + pair, h0:h0 + DN_HEAD_DIM]
            kh = k_s[r0:r0 + pair, h0:h0 + DN_HEAD_DIM]
            vh = v_s[r0:r0 + pair, h0:h0 + DN_HEAD_DIM]
            g_col = gc_p[:, la:la + 1]
            g_row = gc_t[la:la + 1, r0:r0 + pair]
            b_col = beta_p[:, h:h + 1]
            eg_col = e_gc[:, la:la + 1]
            decay = jnp.where(causal, jnp.exp(jnp.where(causal, g_col - g_row, 0.0)), 0.0)
            kb = kh * b_col
            kq = _mm_nt(jnp.concatenate([kb, qh], axis=0), kh)
            lows.append(jnp.where(strict, kq[:pair] * decay, 0.0))
            a_s[r0:r0 + pair, h0:h0 + DN_HEAD_DIM] = (kq[pair:] * decay).astype(BF16)
            rhs.append(jnp.concatenate([vh * b_col, kb * eg_col], axis=1).astype(BF16))
            qd_s[r0:r0 + pair, h0:h0 + DN_HEAD_DIM] = (qh * eg_col).astype(BF16)
            kt_s[r0:r0 + pair, h0:h0 + DN_HEAD_DIM] = (kh * e_tail[:, la:la + 1]).astype(BF16)
    t_invs = _unit_lower_inverses(lows, xor_ij, eye)
    for (p, h), t_inv, r in zip(items, t_invs, rhs):
        r0 = p * pair
        h0 = h * DN_HEAD_DIM
        uw = _mm(t_inv, r)
        u_s[r0:r0 + pair, h0:h0 + DN_HEAD_DIM] = uw[:, :DN_HEAD_DIM]
        w_s[r0:r0 + pair, h0:h0 + DN_HEAD_DIM] = uw[:, DN_HEAD_DIM:].astype(BF16)

    for p in range(n_pairs):
        v_new = [[None, None] for _ in heads]
        q_state = [[None, None] for _ in heads]
        for cc in range(2):
            r0 = p * pair + cc * CHUNK
            s_in = [state[h] for h in heads]
            prods = []
            for h in heads:
                cols = slice(h * DN_HEAD_DIM, (h + 1) * DN_HEAD_DIM)
                lhs = jnp.concatenate([w_s[r0:r0 + CHUNK, cols], qd_s[r0:r0 + CHUNK, cols]], axis=0)
                prods.append(_mm(lhs, s_in[h]))
            for h in heads:
                cols = slice(h * DN_HEAD_DIM, (h + 1) * DN_HEAD_DIM)
                v_new[h][cc] = u_s[r0:r0 + CHUNK, cols] - prods[h][:CHUNK]
                q_state[h][cc] = prods[h][CHUNK:]
            for h in heads:
                cols = slice(h * DN_HEAD_DIM, (h + 1) * DN_HEAD_DIM)
                la = DN_HEADS + h
                state[h] = (s_in[h] * e_last[p][cc][:, la:la + 1]
                            + _mm_tn(kt_s[r0:r0 + CHUNK, cols], v_new[h][cc]))
        r0 = p * pair
        for h in heads:
            cols = slice(h * DN_HEAD_DIM, (h + 1) * DN_HEAD_DIM)
            o = (jnp.concatenate(q_state[h], axis=0)
                 + _mm(a_s[r0:r0 + pair, cols], jnp.concatenate(v_new[h], axis=0)))
            o = o * lax.rsqrt(jnp.mean(o * o, axis=-1, keepdims=True) + NORM_EPS) * normw_ref[...]
            zc = z_s[r0:r0 + pair, cols]
            y_ref[r0:r0 + pair, cols] = (o * _silu(zc)).astype(y_ref.dtype)


def _deltanet_branch(x, wqkv, wz, wba, conv_w, alog_row, dtb_row, normw_row):
    b, s, d = x.shape
    tile = DN_TILE
    const = lambda bi, ti: (0, 0)
    return pl.pallas_call(
        _dn_kernel,
        grid=(b, s // tile),
        in_specs=[
            pl.BlockSpec((None, tile, d), lambda bi, ti: (bi, ti, 0)),
            pl.BlockSpec(wqkv.shape, const),
            pl.BlockSpec(wz.shape, const),
            pl.BlockSpec(wba.shape, const),
            pl.BlockSpec(conv_w.shape, const),
            pl.BlockSpec(alog_row.shape, const),
            pl.BlockSpec(dtb_row.shape, const),
            pl.BlockSpec(normw_row.shape, const),
        ],
        out_specs=pl.BlockSpec((None, tile, DN_WIDTH), lambda bi, ti: (bi, ti, 0)),
        out_shape=jax.ShapeDtypeStruct((b, s, DN_WIDTH), BF16),
        scratch_shapes=[
            pltpu.VMEM((tile + SUBLANES, 3 * DN_WIDTH), F32),
            pltpu.VMEM((tile, DN_WIDTH), F32),
            pltpu.VMEM((tile, DN_WIDTH), F32),
            pltpu.VMEM((tile, DN_WIDTH), F32),
            pltpu.VMEM((tile, DN_WIDTH), F32),
            pltpu.VMEM((tile, DN_WIDTH), F32),
            pltpu.VMEM((tile, DN_WIDTH), BF16),
            pltpu.VMEM((tile, DN_WIDTH), BF16),
            pltpu.VMEM((tile, DN_WIDTH), BF16),
            pltpu.VMEM((tile, DN_WIDTH), BF16),
            pltpu.VMEM((DN_HEADS, DN_HEAD_DIM, DN_HEAD_DIM), F32),
        ],
        compiler_params=pltpu.CompilerParams(
            dimension_semantics=("arbitrary", "arbitrary"), vmem_limit_bytes=VMEM_LIMIT_BYTES),
        name="deltanet_branch",
    )(x, wqkv, wz, wba, conv_w, alog_row, dtb_row, normw_row)


def _swa_kernel(sinks_ref, x_ref, wq_ref, wkv_ref, wz_ref, rope_ref, y_ref, kband, vband):
    t = pl.program_id(1)
    w = WINDOW

    @pl.when(t == 0)
    def _():
        kband[:, 0:w, :] = jnp.zeros((2 * SWA_KV_HEADS, w, LANES), BF16)
        vband[:, 0:w, :] = jnp.zeros((2 * SWA_KV_HEADS, w, LANES), BF16)

    cos_p = rope_ref[:, 0:LANES]
    sin_a = rope_ref[:, LANES:2 * LANES]
    sin_b = rope_ref[:, 2 * LANES:3 * LANES]
    half = ROPE_DIM // 2

    def rope(v):
        return v * cos_p + pltpu.roll(v, LANES - half, 1) * sin_a + pltpu.roll(v, half, 1) * sin_b

    xb = x_ref[...].astype(BF16)
    q = jnp.dot(xb, wq_ref[...], preferred_element_type=F32)
    kv = jnp.dot(xb, wkv_ref[...], preferred_element_type=F32)
    z = jnp.dot(xb, wz_ref[...], preferred_element_type=F32)

    k = rope(kv[:, :LANES])
    v = kv[:, LANES:]
    lane = lax.broadcasted_iota(jnp.int32, (w, LANES), 1)
    lo = lane < SWA_HEAD_DIM
    for src, band in ((k, kband), (v, vband)):
        swapped = pltpu.roll(src, SWA_HEAD_DIM, 1)
        band[0, w:2 * w, :] = jnp.where(lo, src, 0.0).astype(BF16)
        band[1, w:2 * w, :] = jnp.where(lo, 0.0, swapped).astype(BF16)
        band[2, w:2 * w, :] = jnp.where(lo, swapped, 0.0).astype(BF16)
        band[3, w:2 * w, :] = jnp.where(lo, 0.0, src).astype(BF16)

    qi = lax.broadcasted_iota(jnp.int32, (w, 2 * w), 0)
    kj = lax.broadcasted_iota(jnp.int32, (w, 2 * w), 1)
    first_valid = jnp.where(t == 0, w, 0)
    valid = (kj > qi) & (kj <= qi + w) & (kj >= first_valid)
    bias = jnp.where(valid, 0.0, MASK_VALUE).astype(F32)

    group = SWA_Q_HEADS // SWA_KV_HEADS
    for pair in range(SWA_Q_HEADS // 2):
        p0 = pair * LANES
        qp = rope(q[:, p0:p0 + LANES]).astype(BF16)
        acc = None
        for hf in range(2):
            head = 2 * pair + hf
            idx = 2 * (head // group) + hf
            sink = sinks_ref[head]
            s = _mm_nt(qp, kband[idx]) + bias
            m = jnp.maximum(jnp.max(s, axis=-1, keepdims=True), sink)
            p = jnp.exp(s - m)
            denom = jnp.sum(p, axis=-1, keepdims=True) + jnp.exp(sink - m)
            o = _mm(p, vband[idx]) / denom
            acc = o if acc is None else acc + o
        y_ref[:, p0:p0 + LANES] = (acc * _silu(z[:, p0:p0 + LANES])).astype(y_ref.dtype)

    kband[:, 0:w, :] = kband[:, w:2 * w, :]
    vband[:, 0:w, :] = vband[:, w:2 * w, :]


def _swa_branch(x, sinks, wq, wkv, wz, rope_tab):
    b, s, d = x.shape
    w = WINDOW
    const = lambda bi, ti: (0, 0)
    return pl.pallas_call(
        _swa_kernel,
        grid=(b, s // w),
        in_specs=[
            pl.BlockSpec(memory_space=pltpu.SMEM),
            pl.BlockSpec((None, w, d), lambda bi, ti: (bi, ti, 0)),
            pl.BlockSpec(wq.shape, const),
            pl.BlockSpec(wkv.shape, const),
            pl.BlockSpec(wz.shape, const),
            pl.BlockSpec((w, 3 * LANES), lambda bi, ti: (ti, 0)),
        ],
        out_specs=pl.BlockSpec((None, w, SWA_WIDTH), lambda bi, ti: (bi, ti, 0)),
        out_shape=jax.ShapeDtypeStruct((b, s, SWA_WIDTH), BF16),
        scratch_shapes=[
            pltpu.VMEM((2 * SWA_KV_HEADS, 2 * w, LANES), BF16),
            pltpu.VMEM((2 * SWA_KV_HEADS, 2 * w, LANES), BF16),
        ],
        compiler_params=pltpu.CompilerParams(
            dimension_semantics=("arbitrary", "arbitrary"), vmem_limit_bytes=VMEM_LIMIT_BYTES),
        name="swa_branch",
    )(sinks, x, wq, wkv, wz, rope_tab)


def _rope_table(seq):
    half = ROPE_DIM // 2
    inv_freq = ROPE_THETA ** (-jnp.arange(0, ROPE_DIM, 2, dtype=F32) / ROPE_DIM)
    ang = jnp.arange(seq, dtype=F32)[:, None] * inv_freq[None, :]
    cos, sin = jnp.cos(ang), jnp.sin(ang)
    rest = SWA_HEAD_DIM - ROPE_DIM
    ones = jnp.ones((seq, rest), F32)
    zeros = jnp.zeros((seq, rest), F32)
    zh = jnp.zeros((seq, half), F32)
    cos_p = jnp.concatenate([cos, cos, ones], axis=1)
    sin_a = jnp.concatenate([-sin, zh, zeros], axis=1)
    sin_b = jnp.concatenate([zh, sin, zeros], axis=1)
    reps = LANES // SWA_HEAD_DIM
    return jnp.concatenate([jnp.tile(cos_p, (1, reps)), jnp.tile(sin_a, (1, reps)),
                            jnp.tile(sin_b, (1, reps))], axis=1)


def _merge_kernel(alpha, x_ref, ya_ref, yb_ref, wg_ref, wa_ref, wb_ref, wo_ref, lng_ref, lnb_ref, o_ref):
    x = x_ref[...]
    gates = jnp.dot(x.astype(BF16), wg_ref[...], preferred_element_type=F32)
    pa = jnp.dot(ya_ref[...], wa_ref[...], preferred_element_type=F32)
    pb = jnp.dot(yb_ref[...], wb_ref[...], preferred_element_type=F32)
    merged = (jax.nn.sigmoid(gates[:, :D_MODEL]) * pa + jax.nn.sigmoid(gates[:, D_MODEL:]) * pb)
    out = jnp.dot(merged.astype(BF16), wo_ref[...], preferred_element_type=F32)
    r = alpha * x + out
    mu = jnp.mean(r, axis=-1, keepdims=True)
    cen = r - mu
    var = jnp.mean(cen * cen, axis=-1, keepdims=True)
    o_ref[...] = (cen * lax.rsqrt(var + LN_EPS) * lng_ref[...] + lnb_ref[...]).astype(o_ref.dtype)


def _merge(x2, ya2, yb2, wg, wa, wb, wo, lng_row, lnb_row, alpha):
    n, d = x2.shape
    tile = MERGE_TILE
    const = lambda i: (0, 0)
    row = lambda i: (i, 0)
    return pl.pallas_call(
        functools.partial(_merge_kernel, alpha),
        grid=(n // tile,),
        in_specs=[
            pl.BlockSpec((tile, d), row),
            pl.BlockSpec((tile, ya2.shape[1]), row),
            pl.BlockSpec((tile, yb2.shape[1]), row),
            pl.BlockSpec(wg.shape, const),
            pl.BlockSpec(wa.shape, const),
            pl.BlockSpec(wb.shape, const),
            pl.BlockSpec(wo.shape, const),
            pl.BlockSpec(lng_row.shape, const),
            pl.BlockSpec(lnb_row.shape, const),
        ],
        out_specs=pl.BlockSpec((tile, d), row),
        out_shape=jax.ShapeDtypeStruct((n, d), x2.dtype),
        compiler_params=pltpu.CompilerParams(
            dimension_semantics=("arbitrary",), vmem_limit_bytes=VMEM_LIMIT_BYTES),
        name="merge_out_norm",
    )(x2, ya2, yb2, wg, wa, wb, wo, lng_row, lnb_row)


def _lane_row(vec, offset):
    n = vec.shape[0]
    return jnp.pad(vec.astype(F32), (offset, LANES - offset - n)).reshape(1, LANES)


def _layer(x, w_in, conv_w, a_log, dt_bias, dn_norm_w, sinks, w_branch, w_out, ln_g, ln_b, rope_tab, alpha):
    b, s, d = x.shape
    wqkv = w_in[:, _OFF_QKV:_OFF_DN_Z].astype(BF16)
    wz_a = w_in[:, _OFF_DN_Z:_OFF_DN_BA].astype(BF16)
    wba = jnp.pad(w_in[:, _OFF_DN_BA:_OFF_SWA], ((0, 0), (0, LANES - 2 * DN_HEADS))).astype(BF16)
    y_a = _deltanet_branch(x, wqkv, wz_a, wba, conv_w.astype(F32), _lane_row(a_log, DN_HEADS),
                           _lane_row(dt_bias, DN_HEADS), dn_norm_w.astype(F32).reshape(1, DN_HEAD_DIM))

    o = _OFF_SWA
    wq = (w_in[:, o:o + SWA_WIDTH] * (SWA_HEAD_DIM ** -0.5)).astype(BF16)
    wkv = w_in[:, o + SWA_WIDTH:o + SWA_WIDTH + 2 * SWA_KV_WIDTH].astype(BF16)
    wz_b = w_in[:, o + SWA_WIDTH + 2 * SWA_KV_WIDTH:_OFF_GATE].astype(BF16)
    y_b = _swa_branch(x, sinks.astype(F32), wq, wkv, wz_b, rope_tab)

    wg = w_in[:, _OFF_GATE:].astype(BF16)
    out = _merge(x.reshape(b * s, d), y_a.reshape(b * s, DN_WIDTH), y_b.reshape(b * s, SWA_WIDTH),
                 wg, w_branch[0].astype(BF16), w_branch[1].astype(BF16), w_out.astype(BF16),
                 ln_g.astype(F32).reshape(1, d), ln_b.astype(F32).reshape(1, d), alpha)
    return out.reshape(b, s, d)


def kernel(x, w_in, conv_w, a_log, dt_bias, dn_norm_w, sinks, w_branch, w_out, ln_g, ln_b):
    depth = w_in.shape[0]
    alpha = (2.0 * depth) ** 0.25
    rope_tab = _rope_table(x.shape[1])
    for layer in range(depth):
        x = _layer(x, w_in[layer], conv_w[layer], a_log[layer], dt_bias[layer], dn_norm_w[layer],
                   sinks[layer], w_branch[layer], w_out[layer], ln_g[layer], ln_b[layer], rope_tab, alpha)
    return x
```

```python
import functools

import jax
import jax.numpy as jnp
from jax import lax
from jax.experimental import pallas as pl
from jax.experimental.pallas import tpu as pltpu

F32 = jnp.float32
BF16 = jnp.bfloat16

D_MODEL = 1024
DN_HEADS = 4
DN_HEAD_DIM = 128
DN_WIDTH = DN_HEADS * DN_HEAD_DIM
CONV_WIDTH = 4
CHUNK = 64
SWA_Q_HEADS = 8
SWA_KV_HEADS = 2
SWA_HEAD_DIM = 64
SWA_WIDTH = SWA_Q_HEADS * SWA_HEAD_DIM
SWA_KV_WIDTH = SWA_KV_HEADS * SWA_HEAD_DIM
WINDOW = 128
ROPE_THETA = 500000.0
ROPE_DIM = SWA_HEAD_DIM // 4
LN_EPS = 1e-5
NORM_EPS = 1e-6
MASK_VALUE = -1e30

LANES = 128
SUBLANES = 8
VMEM_LIMIT_BYTES = 48 * 1024 * 1024

DN_TILE = 1024
DN_SUB = 256
SWA_TILE = 256
MERGE_TILE = 512

_OFF_QKV = 0
_OFF_DN_Z = 3 * DN_WIDTH
_OFF_DN_BA = _OFF_DN_Z + DN_WIDTH
_OFF_SWA = _OFF_DN_BA + 2 * DN_HEADS
_SWA_COLS = SWA_WIDTH + 2 * SWA_KV_WIDTH + SWA_WIDTH
_OFF_GATE = _OFF_SWA + _SWA_COLS


def _mm(a, b):
    return jnp.dot(a.astype(BF16), b.astype(BF16), preferred_element_type=F32)


def _mm_nt(a, b):
    return lax.dot_general(a.astype(BF16), b.astype(BF16), (((1,), (1,)), ((), ())),
                           preferred_element_type=F32)


def _mm_tn(a, b):
    return lax.dot_general(a.astype(BF16), b.astype(BF16), (((0,), (0,)), ((), ())),
                           preferred_element_type=F32)


def _silu(v):
    return v * jax.nn.sigmoid(v)


def _interleave(streams):
    live = [[gen, 0, steps] for gen, steps in streams]
    while live:
        entry = min(live, key=lambda e: e[1] / e[2])
        try:
            next(entry[0])
            entry[1] += 1
        except StopIteration:
            live.remove(entry)


def _dn_kernel(x_ref, wqkv_ref, wz_ref, wba_ref, convw_ref, alog_ref, dtb_ref, normw_ref, y_ref,
               hbuf, q_s, k_s, v_s, z_s, beta_s, gc_s, gct_s, u_s, w_s, qd_s, kt_s, a_s, state):
    t = pl.program_id(1)
    tile = x_ref.shape[0]
    sub = DN_SUB
    n_sub = tile // sub
    halo = SUBLANES
    pair = 2 * CHUNK
    pairs_per_sub = sub // pair
    heads = range(DN_HEADS)

    @pl.when(t == 0)
    def _():
        hbuf[0:halo, :] = jnp.zeros((halo, 3 * DN_WIDTH), F32)
        state[...] = jnp.zeros_like(state)

    row = lax.broadcasted_iota(jnp.int32, (pair, pair), 0)
    col = lax.broadcasted_iota(jnp.int32, (pair, pair), 1)
    same_chunk = (row // CHUNK) == (col // CHUNK)
    causal = same_chunk & (row >= col)
    strict = same_chunk & (row > col)
    xor_ij = row ^ col
    eye = jnp.where(row == col, 1.0, 0.0).astype(F32)
    first_chunk = lax.broadcasted_iota(jnp.int32, (pair, LANES), 0) < CHUNK
    pos = lax.broadcasted_iota(jnp.int32, (sub, LANES), 0) % CHUNK

    def head_cols(h):
        return slice(h * DN_HEAD_DIM, (h + 1) * DN_HEAD_DIM)

    def front(j):
        r0 = j * sub
        rows = slice(r0, r0 + sub)
        xb = x_ref[rows, :].astype(BF16)
        hbuf[halo + r0:halo + r0 + sub, :] = jnp.dot(xb, wqkv_ref[...], preferred_element_type=F32)
        yield
        z_s[rows, :] = jnp.dot(xb, wz_ref[...], preferred_element_type=F32)
        ba = jnp.dot(xb, wba_ref[...], preferred_element_type=F32)
        yield
        beta_s[rows, :] = jax.nn.sigmoid(ba)
        xg = ba + dtb_ref[...]
        softplus = jnp.maximum(xg, 0.0) + jnp.log1p(jnp.exp(-jnp.abs(xg)))
        gc = -jnp.exp(alog_ref[...]) * softplus
        step = 1
        while step < CHUNK:
            gc = gc + jnp.where(pos >= step, pltpu.roll(gc, step, 0), 0.0)
            step *= 2
        gc_s[rows, :] = gc
        gct_s[:, rows] = gc.T
        yield
        dests = (q_s, k_s, v_s)
        for s in range(3 * DN_HEADS):
            cols = slice(s * DN_HEAD_DIM, (s + 1) * DN_HEAD_DIM)
            acc = None
            for tap in range(CONV_WIDTH):
                h_r0 = halo + r0 - (CONV_WIDTH - 1) + tap
                term = convw_ref[tap:tap + 1, cols] * hbuf[h_r0:h_r0 + sub, cols]
                acc = term if acc is None else acc + term
            a = _silu(acc)
            if s < 2 * DN_HEADS:
                a = a * lax.rsqrt(jnp.sum(a * a, axis=-1, keepdims=True) + NORM_EPS)
                if s < DN_HEADS:
                    a = a * (DN_HEAD_DIM ** -0.5)
            dests[s // DN_HEADS][rows, head_cols(s % DN_HEADS)] = a
            yield
        if j == n_sub - 1:
            hbuf[0:halo, :] = hbuf[tile:tile + halo, :]

    front_steps = 3 + 3 * DN_HEADS

    def factor(j):
        problems = [(j * sub + p * pair, h) for p in range(pairs_per_sub) for h in heads]
        lows, rhs = [], []
        for r0, h in problems:
            rows = slice(r0, r0 + pair)
            la = DN_HEADS + h
            if h == 0:
                gc_p = gc_s[rows, :]
                e_gc = jnp.exp(gc_p)
                g_end = jnp.where(first_chunk, gc_p[CHUNK - 1:CHUNK, :], gc_p[pair - 1:pair, :])
                e_tail = jnp.exp(g_end - gc_p)
                beta_p = beta_s[rows, :]
            g_col = gc_p[:, la:la + 1]
            g_row = gct_s[la:la + 1, rows]
            b_col = beta_p[:, h:h + 1]
            eg_col = e_gc[:, la:la + 1]
            qh = q_s[rows, head_cols(h)]
            kh = k_s[rows, head_cols(h)]
            vh = v_s[rows, head_cols(h)]
            decay = jnp.where(causal, jnp.exp(jnp.where(causal, g_col - g_row, 0.0)), 0.0)
            kb = kh * b_col
            kq = _mm_nt(jnp.concatenate([kb, qh], axis=0), kh)
            lows.append(jnp.where(strict, kq[:pair] * decay, 0.0))
            a_s[rows, head_cols(h)] = (kq[pair:] * decay).astype(BF16)
            rhs.append(jnp.concatenate([vh * b_col, kb * eg_col], axis=1).astype(BF16))
            qd_s[rows, head_cols(h)] = (qh * eg_col).astype(BF16)
            kt_s[rows, head_cols(h)] = (kh * e_tail[:, la:la + 1]).astype(BF16)
            yield
        invs = [eye - jnp.where(xor_ij == 1, low, 0.0) for low in lows]
        level = 1
        while (1 << level) < CHUNK:
            joins = (xor_ij >> level) == 1
            cs = [jnp.where(joins, low, 0.0).astype(BF16) for low in lows]
            xs = [inv.astype(BF16) for inv in invs]
            xcs = [_mm(x, c) for x, c in zip(xs, cs)]
            yield
            invs = [inv - _mm(xc, x) for inv, xc, x in zip(invs, xcs, xs)]
            yield
            level += 1
        for (r0, h), t_inv, r in zip(problems, invs, rhs):
            rows = slice(r0, r0 + pair)
            uw = _mm(t_inv, r)
            u_s[rows, head_cols(h)] = uw[:, :DN_HEAD_DIM]
            w_s[rows, head_cols(h)] = uw[:, DN_HEAD_DIM:].astype(BF16)
            yield

    levels = CHUNK.bit_length() - 2
    factor_steps = 2 * pairs_per_sub * DN_HEADS + 2 * levels

    def recur(j):
        for p in range(pairs_per_sub):
            p0 = j * sub + p * pair
            v_new = [[None, None] for _ in heads]
            q_state = [[None, None] for _ in heads]
            for cc in range(2):
                r0 = p0 + cc * CHUNK
                rows = slice(r0, r0 + CHUNK)
                e_end = jnp.exp(gc_s[r0 + CHUNK - 1:r0 + CHUNK, :])
                s_in = [state[h] for h in heads]
                prods = []
                for h in heads:
                    lhs = jnp.concatenate([w_s[rows, head_cols(h)], qd_s[rows, head_cols(h)]], axis=0)
                    prods.append(_mm(lhs, s_in[h]))
                for h in heads:
                    v_new[h][cc] = u_s[rows, head_cols(h)] - prods[h][:CHUNK]
                    q_state[h][cc] = prods[h][CHUNK:]
                yield
                for h in heads:
                    la = DN_HEADS + h
                    state[h] = (s_in[h] * e_end[:, la:la + 1]
                                + _mm_tn(kt_s[rows, head_cols(h)], v_new[h][cc]))
                yield
            rows = slice(p0, p0 + pair)
            for h in heads:
                o = (jnp.concatenate(q_state[h], axis=0)
                     + _mm(a_s[rows, head_cols(h)], jnp.concatenate(v_new[h], axis=0)))
                o = o * lax.rsqrt(jnp.mean(o * o, axis=-1, keepdims=True) + NORM_EPS) * normw_ref[...]
                y_ref[rows, head_cols(h)] = (o * _silu(z_s[rows, head_cols(h)])).astype(y_ref.dtype)
            yield

    recur_steps = 5 * pairs_per_sub

    stages = ((front, front_steps), (factor, factor_steps), (recur, recur_steps))
    for slot in range(n_sub + len(stages) - 1):
        _interleave([(stage(slot - k), steps) for k, (stage, steps) in enumerate(stages)
                     if 0 <= slot - k < n_sub])


def _deltanet_branch(x, wqkv, wz, wba, conv_w, alog_row, dtb_row, normw_row):
    b, s, d = x.shape
    tile = DN_TILE
    const = lambda bi, ti: (0, 0)
    return pl.pallas_call(
        _dn_kernel,
        grid=(b, s // tile),
        in_specs=[
            pl.BlockSpec((None, tile, d), lambda bi, ti: (bi, ti, 0)),
            pl.BlockSpec(wqkv.shape, const),
            pl.BlockSpec(wz.shape, const),
            pl.BlockSpec(wba.shape, const),
            pl.BlockSpec(conv_w.shape, const),
            pl.BlockSpec(alog_row.shape, const),
            pl.BlockSpec(dtb_row.shape, const),
            pl.BlockSpec(normw_row.shape, const),
        ],
        out_specs=pl.BlockSpec((None, tile, DN_WIDTH), lambda bi, ti: (bi, ti, 0)),
        out_shape=jax.ShapeDtypeStruct((b, s, DN_WIDTH), BF16),
        scratch_shapes=[
            pltpu.VMEM((tile + SUBLANES, 3 * DN_WIDTH), F32),
            pltpu.VMEM((tile, DN_WIDTH), F32),
            pltpu.VMEM((tile, DN_WIDTH), F32),
            pltpu.VMEM((tile, DN_WIDTH), F32),
            pltpu.VMEM((tile, DN_WIDTH), F32),
            pltpu.VMEM((tile, LANES), F32),
            pltpu.VMEM((tile, LANES), F32),
            pltpu.VMEM((LANES, tile), F32),
            pltpu.VMEM((tile, DN_WIDTH), F32),
            pltpu.VMEM((tile, DN_WIDTH), BF16),
            pltpu.VMEM((tile, DN_WIDTH), BF16),
            pltpu.VMEM((tile, DN_WIDTH), BF16),
            pltpu.VMEM((tile, DN_WIDTH), BF16),
            pltpu.VMEM((DN_HEADS, DN_HEAD_DIM, DN_HEAD_DIM), F32),
        ],
        compiler_params=pltpu.CompilerParams(
            dimension_semantics=("arbitrary", "arbitrary"), vmem_limit_bytes=VMEM_LIMIT_BYTES),
        name="deltanet_branch",
    )(x, wqkv, wz, wba, conv_w, alog_row, dtb_row, normw_row)


def _swa_kernel(sinks_ref, x_ref, wq_ref, wkv_ref, wz_ref, rope_ref, y_ref, kband, vband, q_s, z_s):
    t = pl.program_id(1)
    w = WINDOW
    tile = x_ref.shape[0]
    n_blocks = tile // w

    @pl.when(t == 0)
    def _():
        kband[:, 0:w, :] = jnp.zeros((2 * SWA_KV_HEADS, w, LANES), BF16)
        vband[:, 0:w, :] = jnp.zeros((2 * SWA_KV_HEADS, w, LANES), BF16)

    cos_p = rope_ref[:, 0:LANES]
    sin_a = rope_ref[:, LANES:2 * LANES]
    sin_b = rope_ref[:, 2 * LANES:3 * LANES]
    half = ROPE_DIM // 2

    def rope(v):
        return v * cos_p + pltpu.roll(v, LANES - half, 1) * sin_a + pltpu.roll(v, half, 1) * sin_b

    xb = x_ref[...].astype(BF16)
    q = jnp.dot(xb, wq_ref[...], preferred_element_type=F32)
    for pair in range(SWA_Q_HEADS // 2):
        p0 = pair * LANES
        q_s[:, p0:p0 + LANES] = rope(q[:, p0:p0 + LANES]).astype(BF16)
    kv = jnp.dot(xb, wkv_ref[...], preferred_element_type=F32)
    z_s[...] = jnp.dot(xb, wz_ref[...], preferred_element_type=F32)

    k = rope(kv[:, :LANES])
    v = kv[:, LANES:]
    lane = lax.broadcasted_iota(jnp.int32, (tile, LANES), 1)
    lo = lane < SWA_HEAD_DIM
    for src, band in ((k, kband), (v, vband)):
        swapped = pltpu.roll(src, SWA_HEAD_DIM, 1)
        band[0, w:w + tile, :] = jnp.where(lo, src, 0.0).astype(BF16)
        band[1, w:w + tile, :] = jnp.where(lo, 0.0, swapped).astype(BF16)
        band[2, w:w + tile, :] = jnp.where(lo, swapped, 0.0).astype(BF16)
        band[3, w:w + tile, :] = jnp.where(lo, 0.0, src).astype(BF16)

    qi = lax.broadcasted_iota(jnp.int32, (w, 2 * w), 0)
    kj = lax.broadcasted_iota(jnp.int32, (w, 2 * w), 1)
    in_band = (kj > qi) & (kj <= qi + w)
    bias = jnp.where(in_band, 0.0, MASK_VALUE).astype(F32)
    bias_first = jnp.where(in_band & (kj >= jnp.where(t == 0, w, 0)), 0.0, MASK_VALUE).astype(F32)

    group = SWA_Q_HEADS // SWA_KV_HEADS
    for blk in range(n_blocks):
        r0 = blk * w
        blk_bias = bias_first if blk == 0 else bias
        for pair in range(SWA_Q_HEADS // 2):
            p0 = pair * LANES
            qp = q_s[r0:r0 + w, p0:p0 + LANES]
            acc = None
            for hf in range(2):
                head = 2 * pair + hf
                idx = 2 * (head // group) + hf
                sink = sinks_ref[head]
                s = _mm_nt(qp, kband[idx, r0:r0 + 2 * w, :]) + blk_bias
                m = jnp.maximum(jnp.max(s, axis=-1, keepdims=True), sink)
                p = jnp.exp(s - m)
                denom = jnp.sum(p, axis=-1, keepdims=True) + jnp.exp(sink - m)
                o = _mm(p, vband[idx, r0:r0 + 2 * w, :]) / denom
                acc = o if acc is None else acc + o
            y_ref[r0:r0 + w, p0:p0 + LANES] = (
                acc * _silu(z_s[r0:r0 + w, p0:p0 + LANES])).astype(y_ref.dtype)

    kband[:, 0:w, :] = kband[:, tile:tile + w, :]
    vband[:, 0:w, :] = vband[:, tile:tile + w, :]


def _swa_branch(x, sinks, wq, wkv, wz, rope_tab):
    b, s, d = x.shape
    w = WINDOW
    tile = SWA_TILE
    const = lambda bi, ti: (0, 0)
    return pl.pallas_call(
        _swa_kernel,
        grid=(b, s // tile),
        in_specs=[
            pl.BlockSpec(memory_space=pltpu.SMEM),
            pl.BlockSpec((None, tile, d), lambda bi, ti: (bi, ti, 0)),
            pl.BlockSpec(wq.shape, const),
            pl.BlockSpec(wkv.shape, const),
            pl.BlockSpec(wz.shape, const),
            pl.BlockSpec((tile, 3 * LANES), lambda bi, ti: (ti, 0)),
        ],
        out_specs=pl.BlockSpec((None, tile, SWA_WIDTH), lambda bi, ti: (bi, ti, 0)),
        out_shape=jax.ShapeDtypeStruct((b, s, SWA_WIDTH), BF16),
        scratch_shapes=[
            pltpu.VMEM((2 * SWA_KV_HEADS, w + tile, LANES), BF16),
            pltpu.VMEM((2 * SWA_KV_HEADS, w + tile, LANES), BF16),
            pltpu.VMEM((tile, SWA_WIDTH), BF16),
            pltpu.VMEM((tile, SWA_WIDTH), F32),
        ],
        compiler_params=pltpu.CompilerParams(
            dimension_semantics=("arbitrary", "arbitrary"), vmem_limit_bytes=VMEM_LIMIT_BYTES),
        name="swa_branch",
    )(sinks, x, wq, wkv, wz, rope_tab)


def _rope_table(seq):
    half = ROPE_DIM // 2
    inv_freq = ROPE_THETA ** (-jnp.arange(0, ROPE_DIM, 2, dtype=F32) / ROPE_DIM)
    ang = jnp.arange(seq, dtype=F32)[:, None] * inv_freq[None, :]
    cos, sin = jnp.cos(ang), jnp.sin(ang)
    rest = SWA_HEAD_DIM - ROPE_DIM
    ones = jnp.ones((seq, rest), F32)
    zeros = jnp.zeros((seq, rest), F32)
    zh = jnp.zeros((seq, half), F32)
    cos_p = jnp.concatenate([cos, cos, ones], axis=1)
    sin_a = jnp.concatenate([-sin, zh, zeros], axis=1)
    sin_b = jnp.concatenate([zh, sin, zeros], axis=1)
    reps = LANES // SWA_HEAD_DIM
    return jnp.concatenate([jnp.tile(cos_p, (1, reps)), jnp.tile(sin_a, (1, reps)),
                            jnp.tile(sin_b, (1, reps))], axis=1)


def _merge_kernel(alpha, x_ref, ya_ref, yb_ref, wg_ref, wa_ref, wb_ref, wo_ref, lng_ref, lnb_ref, o_ref):
    x = x_ref[...]
    gates = jnp.dot(x.astype(BF16), wg_ref[...], preferred_element_type=F32)
    pa = jnp.dot(ya_ref[...], wa_ref[...], preferred_element_type=F32)
    pb = jnp.dot(yb_ref[...], wb_ref[...], preferred_element_type=F32)
    merged = (jax.nn.sigmoid(gates[:, :D_MODEL]) * pa + jax.nn.sigmoid(gates[:, D_MODEL:]) * pb)
    out = jnp.dot(merged.astype(BF16), wo_ref[...], preferred_element_type=F32)
    r = alpha * x + out
    mu = jnp.mean(r, axis=-1, keepdims=True)
    cen = r - mu
    var = jnp.mean(cen * cen, axis=-1, keepdims=True)
    o_ref[...] = (cen * lax.rsqrt(var + LN_EPS) * lng_ref[...] + lnb_ref[...]).astype(o_ref.dtype)


def _merge(x2, ya2, yb2, wg, wa, wb, wo, lng_row, lnb_row, alpha):
    n, d = x2.shape
    tile = MERGE_TILE
    const = lambda i: (0, 0)
    row = lambda i: (i, 0)
    return pl.pallas_call(
        functools.partial(_merge_kernel, alpha),
        grid=(n // tile,),
        in_specs=[
            pl.BlockSpec((tile, d), row),
            pl.BlockSpec((tile, ya2.shape[1]), row),
            pl.BlockSpec((tile, yb2.shape[1]), row),
            pl.BlockSpec(wg.shape, const),
            pl.BlockSpec(wa.shape, const),
            pl.BlockSpec(wb.shape, const),
            pl.BlockSpec(wo.shape, const),
            pl.BlockSpec(lng_row.shape, const),
            pl.BlockSpec(lnb_row.shape, const),
        ],
        out_specs=pl.BlockSpec((tile, d), row),
        out_shape=jax.ShapeDtypeStruct((n, d), x2.dtype),
        compiler_params=pltpu.CompilerParams(
            dimension_semantics=("arbitrary",), vmem_limit_bytes=VMEM_LIMIT_BYTES),
        name="merge_out_norm",
    )(x2, ya2, yb2, wg, wa, wb, wo, lng_row, lnb_row)


def _lane_row(vec, offset):
    n = vec.shape[0]
    return jnp.pad(vec.astype(F32), (offset, LANES - offset - n)).reshape(1, LANES)


def _layer(x, w_in, conv_w, a_log, dt_bias, dn_norm_w, sinks, w_branch, w_out, ln_g, ln_b, rope_tab, alpha):
    b, s, d = x.shape
    wqkv = w_in[:, _OFF_QKV:_OFF_DN_Z].astype(BF16)
    wz_a = w_in[:, _OFF_DN_Z:_OFF_DN_BA].astype(BF16)
    wba = jnp.pad(w_in[:, _OFF_DN_BA:_OFF_SWA], ((0, 0), (0, LANES - 2 * DN_HEADS))).astype(BF16)
    y_a = _deltanet_branch(x, wqkv, wz_a, wba, conv_w.astype(F32), _lane_row(a_log, DN_HEADS),
                           _lane_row(dt_bias, DN_HEADS), dn_norm_w.astype(F32).reshape(1, DN_HEAD_DIM))

    o = _OFF_SWA
    wq = (w_in[:, o:o + SWA_WIDTH] * (SWA_HEAD_DIM ** -0.5)).astype(BF16)
    wkv = w_in[:, o + SWA_WIDTH:o + SWA_WIDTH + 2 * SWA_KV_WIDTH].astype(BF16)
    wz_b = w_in[:, o + SWA_WIDTH + 2 * SWA_KV_WIDTH:_OFF_GATE].astype(BF16)
    y_b = _swa_branch(x, sinks.astype(F32), wq, wkv, wz_b, rope_tab)

    wg = w_in[:, _OFF_GATE:].astype(BF16)
    out = _merge(x.reshape(b * s, d), y_a.reshape(b * s, DN_WIDTH), y_b.reshape(b * s, SWA_WIDTH),
                 wg, w_branch[0].astype(BF16), w_branch[1].astype(BF16), w_out.astype(BF16),
                 ln_g.astype(F32).reshape(1, d), ln_b.astype(F32).reshape(1, d), alpha)
    return out.reshape(b, s, d)


def kernel(x, w_in, conv_w, a_log, dt_bias, dn_norm_w, sinks, w_branch, w_out, ln_g, ln_b):
    depth = w_in.shape[0]
    alpha = (2.0 * depth) ** 0.25
    rope_tab = _rope_table(x.shape[1])
    for layer in range(depth):
        x = _layer(x, w_in[layer], conv_w[layer], a_log[layer], dt_bias[layer], dn_norm_w[layer],
                   sinks[layer], w_branch[layer], w_out[layer], ln_g[layer], ln_b[layer], rope_tab, alpha)
    return x
```

```python
import functools

import jax
import jax.numpy as jnp
from jax import lax
from jax.experimental import pallas as pl
from jax.experimental.pallas import tpu as pltpu

F32 = jnp.float32
BF16 = jnp.bfloat16

D_MODEL = 1024
DN_HEADS = 4
DN_HEAD_DIM = 128
DN_WIDTH = DN_HEADS * DN_HEAD_DIM
CONV_WIDTH = 4
CHUNK = 64
SWA_Q_HEADS = 8
SWA_KV_HEADS = 2
SWA_HEAD_DIM = 64
SWA_WIDTH = SWA_Q_HEADS * SWA_HEAD_DIM
SWA_KV_WIDTH = SWA_KV_HEADS * SWA_HEAD_DIM
WINDOW = 128
ROPE_THETA = 500000.0
ROPE_DIM = SWA_HEAD_DIM // 4
LN_EPS = 1e-5
NORM_EPS = 1e-6
MASK_VALUE = -1e30

LANES = 128
SUBLANES = 8
VMEM_LIMIT_BYTES = 48 * 1024 * 1024

DN_TILE = 1024
DN_SUB = 256
SWA_TILE = 1024
SWA_SUB = 256
SWA_HEAD_PAIRS_IN_FLIGHT = 4
MERGE_TILE = 1024
MERGE_SUB = 256
assert CONV_WIDTH == 4

_OFF_QKV = 0
_OFF_DN_Z = 3 * DN_WIDTH
_OFF_DN_BA = _OFF_DN_Z + DN_WIDTH
_OFF_SWA = _OFF_DN_BA + 2 * DN_HEADS
_SWA_COLS = SWA_WIDTH + 2 * SWA_KV_WIDTH + SWA_WIDTH
_OFF_GATE = _OFF_SWA + _SWA_COLS


def _mm(a, b):
    return jnp.dot(a.astype(BF16), b.astype(BF16), preferred_element_type=F32)


def _mm_nt(a, b):
    return lax.dot_general(a.astype(BF16), b.astype(BF16), (((1,), (1,)), ((), ())),
                           preferred_element_type=F32)


def _mm_tn(a, b):
    return lax.dot_general(a.astype(BF16), b.astype(BF16), (((0,), (0,)), ((), ())),
                           preferred_element_type=F32)


def _silu(v):
    return v * jax.nn.sigmoid(v)


def _interleave(streams):
    live = [[gen, 0, steps] for gen, steps in streams]
    while live:
        entry = min(live, key=lambda e: e[1] / e[2])
        try:
            next(entry[0])
            entry[1] += 1
        except StopIteration:
            live.remove(entry)


def _dn_kernel(x_ref, wqkv_ref, wz_ref, wba_ref, convw_ref, alog_ref, dtb_ref, normw_ref, y_ref,
               hbuf, q_s, k_s, v_s, z_s, beta_s, gc_s, gct_s, u_s, w_s, qd_s, kt_s, a_s, state):
    t = pl.program_id(1)
    tile = x_ref.shape[0]
    sub = DN_SUB
    n_sub = tile // sub
    halo = SUBLANES
    pair = 2 * CHUNK
    pairs_per_sub = sub // pair
    heads = range(DN_HEADS)

    @pl.when(t == 0)
    def _():
        hbuf[0:halo, :] = jnp.zeros((halo, 3 * DN_WIDTH), F32)
        state[...] = jnp.zeros_like(state)

    row = lax.broadcasted_iota(jnp.int32, (CHUNK, pair), 0)
    lane = lax.broadcasted_iota(jnp.int32, (CHUNK, pair), 1)
    col = lane % CHUNK
    left = lane < CHUNK
    causal = row >= col
    strict = row > col
    xor_ij = row ^ col
    eye = jnp.where(row == col, 1.0, 0.0).astype(F32)
    first_chunk = lax.broadcasted_iota(jnp.int32, (pair, LANES), 0) < CHUNK
    pos = lax.broadcasted_iota(jnp.int32, (sub, LANES), 0) % CHUNK

    def head_cols(h):
        return slice(h * DN_HEAD_DIM, (h + 1) * DN_HEAD_DIM)

    def side_by_side(m):
        return jnp.where(left, m[:CHUNK], m[CHUNK:])

    def block_diag(m):
        zero = jnp.zeros_like(m)
        return jnp.concatenate([jnp.where(left, m, zero), jnp.where(left, zero, m)], axis=0)

    def front(j):
        r0 = j * sub
        rows = slice(r0, r0 + sub)
        xb = x_ref[rows, :].astype(BF16)
        hbuf[halo + r0:halo + r0 + sub, :] = jnp.dot(xb, wqkv_ref[...], preferred_element_type=F32)
        yield
        z_s[rows, :] = jnp.dot(xb, wz_ref[...], preferred_element_type=F32)
        ba = jnp.dot(xb, wba_ref[...], preferred_element_type=F32)
        yield
        beta_s[rows, :] = jax.nn.sigmoid(ba)
        xg = ba + dtb_ref[...]
        softplus = jnp.maximum(xg, 0.0) + jnp.log1p(jnp.exp(-jnp.abs(xg)))
        gc = -jnp.exp(alog_ref[...]) * softplus
        step = 1
        while step < CHUNK:
            gc = gc + jnp.where(pos >= step, pltpu.roll(gc, step, 0), 0.0)
            step *= 2
        gc_s[rows, :] = gc
        gct_s[:, rows] = gc.T
        yield
        dests = (q_s, k_s, v_s)
        for s in range(3 * DN_HEADS):
            cols = slice(s * DN_HEAD_DIM, (s + 1) * DN_HEAD_DIM)
            ext = hbuf[r0:r0 + halo + sub, cols]
            prev = pltpu.roll(ext, 1, 0)
            older = convw_ref[0:1, cols] * prev + convw_ref[1:2, cols] * ext
            newer = convw_ref[2:3, cols] * prev + convw_ref[3:4, cols] * ext
            acc = (pltpu.roll(older, 2, 0) + newer)[halo:, :]
            a = _silu(acc)
            if s < 2 * DN_HEADS:
                a = a * lax.rsqrt(jnp.sum(a * a, axis=-1, keepdims=True) + NORM_EPS)
                if s < DN_HEADS:
                    a = a * (DN_HEAD_DIM ** -0.5)
            dests[s // DN_HEADS][rows, head_cols(s % DN_HEADS)] = a
            yield
        if j == n_sub - 1:
            hbuf[0:halo, :] = hbuf[tile:tile + halo, :]

    front_steps = 3 + 3 * DN_HEADS

    def factor(j):
        problems = [(j * sub + p * pair, h) for p in range(pairs_per_sub) for h in heads]
        lows, rhs = [], []
        for r0, h in problems:
            rows = slice(r0, r0 + pair)
            la = DN_HEADS + h
            if h == 0:
                gc_p = gc_s[rows, :]
                e_gc = jnp.exp(gc_p)
                g_end = jnp.where(first_chunk, gc_p[CHUNK - 1:CHUNK, :], gc_p[pair - 1:pair, :])
                e_tail = jnp.exp(g_end - gc_p)
                beta_p = beta_s[rows, :]
            g_col = gc_p[:, la:la + 1]
            g_row = gct_s[la:la + 1, rows]
            b_col = beta_p[:, h:h + 1]
            eg_col = e_gc[:, la:la + 1]
            qh = q_s[rows, head_cols(h)]
            kh = k_s[rows, head_cols(h)]
            vh = v_s[rows, head_cols(h)]
            g_diff = jnp.where(left, g_col[:CHUNK], g_col[CHUNK:]) - g_row
            decay = jnp.where(causal, jnp.exp(jnp.where(causal, g_diff, 0.0)), 0.0)
            kb = kh * b_col
            kq = _mm_nt(jnp.concatenate([kb, qh], axis=0), kh)
            lows.append(jnp.where(strict, side_by_side(kq[:pair]) * decay, 0.0))
            a_s[rows, head_cols(h)] = block_diag((side_by_side(kq[pair:]) * decay).astype(BF16))
            rhs.append(jnp.concatenate([vh * b_col, kb * eg_col], axis=1).astype(BF16))
            qd_s[rows, head_cols(h)] = (qh * eg_col).astype(BF16)
            kt_s[rows, head_cols(h)] = (kh * e_tail[:, la:la + 1]).astype(BF16)
            yield
        invs = [eye - jnp.where(xor_ij == 1, low, 0.0) for low in lows]
        level = 1
        while (1 << level) < CHUNK:
            joins = (xor_ij >> level) == 1
            cs = [block_diag(jnp.where(joins, low, 0.0).astype(BF16)) for low in lows]
            xs = [inv.astype(BF16) for inv in invs]
            xcs = [_mm(x, c) for x, c in zip(xs, cs)]
            yield
            invs = [inv - _mm(xc, block_diag(x)) for inv, xc, x in zip(invs, xcs, xs)]
            yield
            level += 1
        for (r0, h), t_inv, r in zip(problems, invs, rhs):
            rows = slice(r0, r0 + pair)
            uw = _mm(block_diag(t_inv.astype(BF16)), r)
            u_s[rows, head_cols(h)] = uw[:, :DN_HEAD_DIM]
            w_s[rows, head_cols(h)] = uw[:, DN_HEAD_DIM:].astype(BF16)
            yield

    levels = CHUNK.bit_length() - 2
    factor_steps = 2 * pairs_per_sub * DN_HEADS + 2 * levels

    def recur(j):
        for p in range(pairs_per_sub):
            p0 = j * sub + p * pair
            v_new = [[None, None] for _ in heads]
            q_state = [[None, None] for _ in heads]
            for cc in range(2):
                r0 = p0 + cc * CHUNK
                rows = slice(r0, r0 + CHUNK)
                e_end = jnp.exp(gc_s[r0 + CHUNK - 1:r0 + CHUNK, :])
                s_in = [state[h] for h in heads]
                prods = []
                for h in heads:
                    lhs = jnp.concatenate([w_s[rows, head_cols(h)], qd_s[rows, head_cols(h)]], axis=0)
                    prods.append(_mm(lhs, s_in[h]))
                for h in heads:
                    v_new[h][cc] = u_s[rows, head_cols(h)] - prods[h][:CHUNK]
                    q_state[h][cc] = prods[h][CHUNK:]
                yield
                for h in heads:
                    la = DN_HEADS + h
                    state[h] = (s_in[h] * e_end[:, la:la + 1]
                                + _mm_tn(kt_s[rows, head_cols(h)], v_new[h][cc]))
                yield
            rows = slice(p0, p0 + pair)
            for h in heads:
                o = (jnp.concatenate(q_state[h], axis=0)
                     + _mm(a_s[rows, head_cols(h)], jnp.concatenate(v_new[h], axis=0)))
                o = o * lax.rsqrt(jnp.mean(o * o, axis=-1, keepdims=True) + NORM_EPS) * normw_ref[...]
                y_ref[rows, head_cols(h)] = (o * _silu(z_s[rows, head_cols(h)])).astype(y_ref.dtype)
            yield

    recur_steps = 5 * pairs_per_sub

    stages = ((front, front_steps), (factor, factor_steps), (recur, recur_steps))
    for slot in range(n_sub + len(stages) - 1):
        _interleave([(stage(slot - k), steps) for k, (stage, steps) in enumerate(stages)
                     if 0 <= slot - k < n_sub])


def _deltanet_branch(x, wqkv, wz, wba, conv_w, alog_row, dtb_row, normw_row):
    b, s, d = x.shape
    tile = DN_TILE
    const = lambda bi, ti: (0, 0)
    return pl.pallas_call(
        _dn_kernel,
        grid=(b, s // tile),
        in_specs=[
            pl.BlockSpec((None, tile, d), lambda bi, ti: (bi, ti, 0)),
            pl.BlockSpec(wqkv.shape, const),
            pl.BlockSpec(wz.shape, const),
            pl.BlockSpec(wba.shape, const),
            pl.BlockSpec(conv_w.shape, const),
            pl.BlockSpec(alog_row.shape, const),
            pl.BlockSpec(dtb_row.shape, const),
            pl.BlockSpec(normw_row.shape, const),
        ],
        out_specs=pl.BlockSpec((None, tile, DN_WIDTH), lambda bi, ti: (bi, ti, 0)),
        out_shape=jax.ShapeDtypeStruct((b, s, DN_WIDTH), BF16),
        scratch_shapes=[
            pltpu.VMEM((tile + SUBLANES, 3 * DN_WIDTH), F32),
            pltpu.VMEM((tile, DN_WIDTH), F32),
            pltpu.VMEM((tile, DN_WIDTH), F32),
            pltpu.VMEM((tile, DN_WIDTH), F32),
            pltpu.VMEM((tile, DN_WIDTH), F32),
            pltpu.VMEM((tile, LANES), F32),
            pltpu.VMEM((tile, LANES), F32),
            pltpu.VMEM((LANES, tile), F32),
            pltpu.VMEM((tile, DN_WIDTH), F32),
            pltpu.VMEM((tile, DN_WIDTH), BF16),
            pltpu.VMEM((tile, DN_WIDTH), BF16),
            pltpu.VMEM((tile, DN_WIDTH), BF16),
            pltpu.VMEM((tile, DN_WIDTH), BF16),
            pltpu.VMEM((DN_HEADS, DN_HEAD_DIM, DN_HEAD_DIM), F32),
        ],
        compiler_params=pltpu.CompilerParams(
            dimension_semantics=("arbitrary", "arbitrary"), vmem_limit_bytes=VMEM_LIMIT_BYTES),
        name="deltanet_branch",
    )(x, wqkv, wz, wba, conv_w, alog_row, dtb_row, normw_row)


def _swa_kernel(sinks_ref, x_ref, wq_ref, wkv_ref, wz_ref, rope_ref, y_ref, kband, vband, q_s, z_s):
    t = pl.program_id(1)
    w = WINDOW
    tile = x_ref.shape[0]

    @pl.when(t == 0)
    def _():
        kband[:, 0:w, :] = jnp.zeros((2 * SWA_KV_HEADS, w, LANES), BF16)
        vband[:, 0:w, :] = jnp.zeros((2 * SWA_KV_HEADS, w, LANES), BF16)

    half = ROPE_DIM // 2
    sub = SWA_SUB
    n_sub = tile // sub
    n_pairs = SWA_Q_HEADS // 2
    group = SWA_Q_HEADS // SWA_KV_HEADS
    lo = lax.broadcasted_iota(jnp.int32, (sub, LANES), 1) < SWA_HEAD_DIM

    qi = lax.broadcasted_iota(jnp.int32, (w, 2 * w), 0)
    kj = lax.broadcasted_iota(jnp.int32, (w, 2 * w), 1)
    in_band = (kj > qi) & (kj <= qi + w)
    bias = jnp.where(in_band, 0.0, MASK_VALUE).astype(F32)
    bias_first = jnp.where(in_band & (kj >= jnp.where(t == 0, w, 0)), 0.0, MASK_VALUE).astype(F32)

    def front(j):
        r0 = j * sub
        rows = slice(r0, r0 + sub)
        cos_p = rope_ref[rows, 0:LANES]
        sin_a = rope_ref[rows, LANES:2 * LANES]
        sin_b = rope_ref[rows, 2 * LANES:3 * LANES]

        def rope(v):
            return v * cos_p + pltpu.roll(v, LANES - half, 1) * sin_a + pltpu.roll(v, half, 1) * sin_b

        xb = x_ref[rows, :].astype(BF16)
        q = jnp.dot(xb, wq_ref[...], preferred_element_type=F32)
        yield
        for pair in range(n_pairs):
            p0 = pair * LANES
            q_s[rows, p0:p0 + LANES] = rope(q[:, p0:p0 + LANES]).astype(BF16)
        yield
        kv = jnp.dot(xb, wkv_ref[...], preferred_element_type=F32)
        z_s[rows, :] = jnp.dot(xb, wz_ref[...], preferred_element_type=F32)
        yield
        k = rope(kv[:, :LANES])
        v = kv[:, LANES:]
        band_rows = slice(w + r0, w + r0 + sub)
        for src, band in ((k, kband), (v, vband)):
            swapped = pltpu.roll(src, SWA_HEAD_DIM, 1)
            band[0, band_rows, :] = jnp.where(lo, src, 0.0).astype(BF16)
            band[1, band_rows, :] = jnp.where(lo, 0.0, swapped).astype(BF16)
            band[2, band_rows, :] = jnp.where(lo, swapped, 0.0).astype(BF16)
            band[3, band_rows, :] = jnp.where(lo, 0.0, src).astype(BF16)
        yield

    front_steps = 4

    def head_pair(r0, pair, blk_bias):
        p0 = pair * LANES
        qp = q_s[r0:r0 + w, p0:p0 + LANES]
        scores, sinks = [], []
        for hf in range(2):
            head = 2 * pair + hf
            idx = 2 * (head // group) + hf
            sinks.append(sinks_ref[head])
            scores.append(_mm_nt(qp, kband[idx, r0:r0 + 2 * w, :]) + blk_bias)
        yield
        probs, denoms = [], []
        for s, sink in zip(scores, sinks):
            m = jnp.maximum(jnp.max(jnp.maximum(s[:, :w], s[:, w:]), axis=-1, keepdims=True), sink)
            p = jnp.exp(s - m)
            denoms.append(jnp.sum(p[:, :w] + p[:, w:], axis=-1, keepdims=True) + jnp.exp(sink - m))
            probs.append(p.astype(BF16))
        yield
        acc = None
        for hf in range(2):
            idx = 2 * ((2 * pair + hf) // group) + hf
            o = _mm(probs[hf], vband[idx, r0:r0 + 2 * w, :]) / denoms[hf]
            acc = o if acc is None else acc + o
        y_ref[r0:r0 + w, p0:p0 + LANES] = (
            acc * _silu(z_s[r0:r0 + w, p0:p0 + LANES])).astype(y_ref.dtype)
        yield

    def attend(j):
        items = [(j * sub + b * w, pair) for b in range(sub // w) for pair in range(n_pairs)]
        for g0 in range(0, len(items), SWA_HEAD_PAIRS_IN_FLIGHT):
            streams = [head_pair(r0, pair, bias_first if r0 == 0 else bias)
                       for r0, pair in items[g0:g0 + SWA_HEAD_PAIRS_IN_FLIGHT]]
            for _ in range(3):
                for stream in streams:
                    next(stream)
                yield

    attend_steps = 3 * (sub // w) * n_pairs // SWA_HEAD_PAIRS_IN_FLIGHT

    stages = ((front, front_steps), (attend, attend_steps))
    for slot in range(n_sub + len(stages) - 1):
        _interleave([(stage(slot - k), steps) for k, (stage, steps) in enumerate(stages)
                     if 0 <= slot - k < n_sub])

    kband[:, 0:w, :] = kband[:, tile:tile + w, :]
    vband[:, 0:w, :] = vband[:, tile:tile + w, :]


def _swa_branch(x, sinks, wq, wkv, wz, rope_tab):
    b, s, d = x.shape
    w = WINDOW
    tile = SWA_TILE
    const = lambda bi, ti: (0, 0)
    return pl.pallas_call(
        _swa_kernel,
        grid=(b, s // tile),
        in_specs=[
            pl.BlockSpec(memory_space=pltpu.SMEM),
            pl.BlockSpec((None, tile, d), lambda bi, ti: (bi, ti, 0)),
            pl.BlockSpec(wq.shape, const),
            pl.BlockSpec(wkv.shape, const),
            pl.BlockSpec(wz.shape, const),
            pl.BlockSpec((tile, 3 * LANES), lambda bi, ti: (ti, 0)),
        ],
        out_specs=pl.BlockSpec((None, tile, SWA_WIDTH), lambda bi, ti: (bi, ti, 0)),
        out_shape=jax.ShapeDtypeStruct((b, s, SWA_WIDTH), BF16),
        scratch_shapes=[
            pltpu.VMEM((2 * SWA_KV_HEADS, w + tile, LANES), BF16),
            pltpu.VMEM((2 * SWA_KV_HEADS, w + tile, LANES), BF16),
            pltpu.VMEM((tile, SWA_WIDTH), BF16),
            pltpu.VMEM((tile, SWA_WIDTH), F32),
        ],
        compiler_params=pltpu.CompilerParams(
            dimension_semantics=("arbitrary", "arbitrary"), vmem_limit_bytes=VMEM_LIMIT_BYTES),
        name="swa_branch",
    )(sinks, x, wq, wkv, wz, rope_tab)


def _rope_table(seq):
    half = ROPE_DIM // 2
    inv_freq = ROPE_THETA ** (-jnp.arange(0, ROPE_DIM, 2, dtype=F32) / ROPE_DIM)
    ang = jnp.arange(seq, dtype=F32)[:, None] * inv_freq[None, :]
    cos, sin = jnp.cos(ang), jnp.sin(ang)
    rest = SWA_HEAD_DIM - ROPE_DIM
    ones = jnp.ones((seq, rest), F32)
    zeros = jnp.zeros((seq, rest), F32)
    zh = jnp.zeros((seq, half), F32)
    cos_p = jnp.concatenate([cos, cos, ones], axis=1)
    sin_a = jnp.concatenate([-sin, zh, zeros], axis=1)
    sin_b = jnp.concatenate([zh, sin, zeros], axis=1)
    reps = LANES // SWA_HEAD_DIM
    return jnp.concatenate([jnp.tile(cos_p, (1, reps)), jnp.tile(sin_a, (1, reps)),
                            jnp.tile(sin_b, (1, reps))], axis=1)


def _merge_kernel(alpha, x_ref, ya_ref, yb_ref, wg_ref, wa_ref, wb_ref, wo_ref, lng_ref, lnb_ref, o_ref,
                  merged_s):
    tile = x_ref.shape[0]
    sub = MERGE_SUB
    n_sub = tile // sub

    def gated_merge(j):
        rows = slice(j * sub, (j + 1) * sub)
        xb = x_ref[rows, :].astype(BF16)
        gate_a = jax.nn.sigmoid(jnp.dot(xb, wg_ref[:, :D_MODEL], preferred_element_type=F32))
        pa = jnp.dot(ya_ref[rows, :], wa_ref[...], preferred_element_type=F32)
        yield
        gate_b = jax.nn.sigmoid(jnp.dot(xb, wg_ref[:, D_MODEL:], preferred_element_type=F32))
        pb = jnp.dot(yb_ref[rows, :], wb_ref[...], preferred_element_type=F32)
        yield
        merged_s[rows, :] = (gate_a * pa + gate_b * pb).astype(BF16)
        yield

    def project_norm(j):
        rows = slice(j * sub, (j + 1) * sub)
        out = jnp.dot(merged_s[rows, :], wo_ref[...], preferred_element_type=F32)
        yield
        r = alpha * x_ref[rows, :] + out
        mu = jnp.mean(r, axis=-1, keepdims=True)
        cen = r - mu
        var = jnp.mean(cen * cen, axis=-1, keepdims=True)
        o_ref[rows, :] = (cen * lax.rsqrt(var + LN_EPS) * lng_ref[...] + lnb_ref[...]).astype(o_ref.dtype)
        yield

    stages = ((gated_merge, 3), (project_norm, 2))
    for slot in range(n_sub + len(stages) - 1):
        _interleave([(stage(slot - k), steps) for k, (stage, steps) in enumerate(stages)
                     if 0 <= slot - k < n_sub])


def _merge(x2, ya2, yb2, wg, wa, wb, wo, lng_row, lnb_row, alpha):
    n, d = x2.shape
    tile = MERGE_TILE
    const = lambda i: (0, 0)
    row = lambda i: (i, 0)
    return pl.pallas_call(
        functools.partial(_merge_kernel, alpha),
        grid=(n // tile,),
        in_specs=[
            pl.BlockSpec((tile, d), row),
            pl.BlockSpec((tile, ya2.shape[1]), row),
            pl.BlockSpec((tile, yb2.shape[1]), row),
            pl.BlockSpec(wg.shape, const),
            pl.BlockSpec(wa.shape, const),
            pl.BlockSpec(wb.shape, const),
            pl.BlockSpec(wo.shape, const),
            pl.BlockSpec(lng_row.shape, const),
            pl.BlockSpec(lnb_row.shape, const),
        ],
        out_specs=pl.BlockSpec((tile, d), row),
        out_shape=jax.ShapeDtypeStruct((n, d), x2.dtype),
        scratch_shapes=[pltpu.VMEM((tile, d), BF16)],
        compiler_params=pltpu.CompilerParams(
            dimension_semantics=("arbitrary",), vmem_limit_bytes=VMEM_LIMIT_BYTES),
        name="merge_out_norm",
    )(x2, ya2, yb2, wg, wa, wb, wo, lng_row, lnb_row)


def _lane_row(vec, offset):
    n = vec.shape[0]
    return jnp.pad(vec.astype(F32), (offset, LANES - offset - n)).reshape(1, LANES)


def _layer(x, w_in, conv_w, a_log, dt_bias, dn_norm_w, sinks, w_branch, w_out, ln_g, ln_b, rope_tab, alpha):
    b, s, d = x.shape
    wqkv = w_in[:, _OFF_QKV:_OFF_DN_Z].astype(BF16)
    wz_a = w_in[:, _OFF_DN_Z:_OFF_DN_BA].astype(BF16)
    wba = jnp.pad(w_in[:, _OFF_DN_BA:_OFF_SWA], ((0, 0), (0, LANES - 2 * DN_HEADS))).astype(BF16)
    y_a = _deltanet_branch(x, wqkv, wz_a, wba, conv_w.astype(F32), _lane_row(a_log, DN_HEADS),
                           _lane_row(dt_bias, DN_HEADS), dn_norm_w.astype(F32).reshape(1, DN_HEAD_DIM))

    o = _OFF_SWA
    wq = (w_in[:, o:o + SWA_WIDTH] * (SWA_HEAD_DIM ** -0.5)).astype(BF16)
    wkv = w_in[:, o + SWA_WIDTH:o + SWA_WIDTH + 2 * SWA_KV_WIDTH].astype(BF16)
    wz_b = w_in[:, o + SWA_WIDTH + 2 * SWA_KV_WIDTH:_OFF_GATE].astype(BF16)
    y_b = _swa_branch(x, sinks.astype(F32), wq, wkv, wz_b, rope_tab)

    wg = w_in[:, _OFF_GATE:].astype(BF16)
    out = _merge(x.reshape(b * s, d), y_a.reshape(b * s, DN_WIDTH), y_b.reshape(b * s, SWA_WIDTH),
                 wg, w_branch[0].astype(BF16), w_branch[1].astype(BF16), w_out.astype(BF16),
                 ln_g.astype(F32).reshape(1, d), ln_b.astype(F32).reshape(1, d), alpha)
    return out.reshape(b, s, d)


def kernel(x, w_in, conv_w, a_log, dt_bias, dn_norm_w, sinks, w_branch, w_out, ln_g, ln_b):
    depth = w_in.shape[0]
    alpha = (2.0 * depth) ** 0.25
    rope_tab = _rope_table(x.shape[1])
    for layer in range(depth):
        x = _layer(x, w_in[layer], conv_w[layer], a_log[layer], dt_bias[layer], dn_norm_w[layer],
                   sinks[layer], w_branch[layer], w_out[layer], ln_g[layer], ln_b[layer], rope_tab, alpha)
    return x
```

```python
import functools

import jax
import jax.numpy as jnp
from jax import lax
from jax.experimental import pallas as pl
from jax.experimental.pallas import tpu as pltpu

F32 = jnp.float32
BF16 = jnp.bfloat16

D_MODEL = 1024
DN_HEADS = 4
DN_HEAD_DIM = 128
DN_WIDTH = DN_HEADS * DN_HEAD_DIM
CONV_WIDTH = 4
CHUNK = 64
SWA_Q_HEADS = 8
SWA_KV_HEADS = 2
SWA_HEAD_DIM = 64
SWA_WIDTH = SWA_Q_HEADS * SWA_HEAD_DIM
SWA_KV_WIDTH = SWA_KV_HEADS * SWA_HEAD_DIM
WINDOW = 128
ROPE_THETA = 500000.0
ROPE_DIM = SWA_HEAD_DIM // 4
LN_EPS = 1e-5
NORM_EPS = 1e-6
MASK_VALUE = -1e30

LANES = 128
SUBLANES = 8
VMEM_LIMIT_BYTES = 48 * 1024 * 1024

DN_TILE = 1024
DN_SUB = 256
SWA_TILE = 1024
SWA_SUB = 256
SWA_GROUPS_IN_FLIGHT = 2
MERGE_TILE = 1024
MERGE_SUB = 256
assert CONV_WIDTH == 4

_OFF_QKV = 0
_OFF_DN_Z = 3 * DN_WIDTH
_OFF_DN_BA = _OFF_DN_Z + DN_WIDTH
_OFF_SWA = _OFF_DN_BA + 2 * DN_HEADS
_SWA_COLS = SWA_WIDTH + 2 * SWA_KV_WIDTH + SWA_WIDTH
_OFF_GATE = _OFF_SWA + _SWA_COLS


def _make_w_all_layout():
    widths = (("dn_qkv", 3 * DN_WIDTH), ("dn_z", DN_WIDTH), ("gates", 2 * D_MODEL), ("swa_q", SWA_WIDTH),
              ("swa_z", SWA_WIDTH), ("swa_kv", 2 * SWA_KV_WIDTH), ("dn_ba", LANES))
    layout, off = {}, 0
    for name, width in widths:
        assert off % width == 0
        layout[name] = (off, width)
        off += width
    return layout


_W_ALL_LAYOUT = _make_w_all_layout()


def _mm(a, b):
    return jnp.dot(a.astype(BF16), b.astype(BF16), preferred_element_type=F32)


def _mm_nt(a, b):
    return lax.dot_general(a.astype(BF16), b.astype(BF16), (((1,), (1,)), ((), ())),
                           preferred_element_type=F32)


def _mm_tn(a, b):
    return lax.dot_general(a.astype(BF16), b.astype(BF16), (((0,), (0,)), ((), ())),
                           preferred_element_type=F32)


def _silu(v):
    return v * jax.nn.sigmoid(v)


def _interleave(streams):
    live = [[gen, 0, steps] for gen, steps in streams]
    while live:
        entry = min(live, key=lambda e: e[1] / e[2])
        try:
            next(entry[0])
            entry[1] += 1
        except StopIteration:
            live.remove(entry)


def _dn_kernel(x_ref, wqkv_ref, wz_ref, wba_ref, convw_ref, alog_ref, dtb_ref, normw_ref, y_ref,
               hbuf, q_s, k_s, v_s, z_s, beta_s, gc_s, gct_s, u_s, w_s, qd_s, kt_s, a_s, state):
    t = pl.program_id(1)
    tile = x_ref.shape[0]
    sub = DN_SUB
    n_sub = tile // sub
    halo = SUBLANES
    pair = 2 * CHUNK
    pairs_per_sub = sub // pair
    heads = range(DN_HEADS)

    @pl.when(t == 0)
    def _():
        hbuf[0:halo, :] = jnp.zeros((halo, 3 * DN_WIDTH), F32)
        state[...] = jnp.zeros_like(state)

    row = lax.broadcasted_iota(jnp.int32, (CHUNK, pair), 0)
    lane = lax.broadcasted_iota(jnp.int32, (CHUNK, pair), 1)
    col = lane % CHUNK
    left = lane < CHUNK
    causal = row >= col
    strict = row > col
    xor_ij = row ^ col
    eye = jnp.where(row == col, 1.0, 0.0).astype(F32)
    first_chunk = lax.broadcasted_iota(jnp.int32, (pair, LANES), 0) < CHUNK
    pos = lax.broadcasted_iota(jnp.int32, (sub, LANES), 0) % CHUNK

    def head_cols(h):
        return slice(h * DN_HEAD_DIM, (h + 1) * DN_HEAD_DIM)

    def side_by_side(m):
        return jnp.where(left, m[:CHUNK], m[CHUNK:])

    def block_diag(m):
        zero = jnp.zeros_like(m)
        return jnp.concatenate([jnp.where(left, m, zero), jnp.where(left, zero, m)], axis=0)

    def front(j):
        r0 = j * sub
        rows = slice(r0, r0 + sub)
        xb = x_ref[rows, :].astype(BF16)
        hbuf[halo + r0:halo + r0 + sub, :] = jnp.dot(xb, wqkv_ref[...], preferred_element_type=F32)
        yield
        z_s[rows, :] = jnp.dot(xb, wz_ref[...], preferred_element_type=F32)
        ba = jnp.dot(xb, wba_ref[...], preferred_element_type=F32)
        yield
        beta_s[rows, :] = jax.nn.sigmoid(ba)
        xg = ba + dtb_ref[...]
        softplus = jnp.maximum(xg, 0.0) + jnp.log1p(jnp.exp(-jnp.abs(xg)))
        gc = -jnp.exp(alog_ref[...]) * softplus
        step = 1
        while step < CHUNK:
            gc = gc + jnp.where(pos >= step, pltpu.roll(gc, step, 0), 0.0)
            step *= 2
        gc_s[rows, :] = gc
        gct_s[:, rows] = gc.T
        yield
        dests = (q_s, k_s, v_s)
        for s in range(3 * DN_HEADS):
            cols = slice(s * DN_HEAD_DIM, (s + 1) * DN_HEAD_DIM)
            ext = hbuf[r0:r0 + halo + sub, cols]
            prev = pltpu.roll(ext, 1, 0)
            older = convw_ref[0:1, cols] * prev + convw_ref[1:2, cols] * ext
            newer = convw_ref[2:3, cols] * prev + convw_ref[3:4, cols] * ext
            acc = (pltpu.roll(older, 2, 0) + newer)[halo:, :]
            yield
            a = _silu(acc)
            if s < 2 * DN_HEADS:
                a = a * lax.rsqrt(jnp.sum(a * a, axis=-1, keepdims=True) + NORM_EPS)
                if s < DN_HEADS:
                    a = a * (DN_HEAD_DIM ** -0.5)
            dests[s // DN_HEADS][rows, head_cols(s % DN_HEADS)] = a
            yield
        if j == n_sub - 1:
            hbuf[0:halo, :] = hbuf[tile:tile + halo, :]

    front_steps = 3 + 2 * 3 * DN_HEADS

    def factor(j):
        problems = [(j * sub + p * pair, h) for p in range(pairs_per_sub) for h in heads]
        lows, rhs = [], []
        for r0, h in problems:
            rows = slice(r0, r0 + pair)
            la = DN_HEADS + h
            if h == 0:
                gc_p = gc_s[rows, :]
                e_gc = jnp.exp(gc_p)
                g_end = jnp.where(first_chunk, gc_p[CHUNK - 1:CHUNK, :], gc_p[pair - 1:pair, :])
                e_tail = jnp.exp(g_end - gc_p)
                beta_p = beta_s[rows, :]
            g_col = gc_p[:, la:la + 1]
            g_row = gct_s[la:la + 1, rows]
            b_col = beta_p[:, h:h + 1]
            eg_col = e_gc[:, la:la + 1]
            qh = q_s[rows, head_cols(h)]
            kh = k_s[rows, head_cols(h)]
            vh = v_s[rows, head_cols(h)]
            g_diff = jnp.where(left, g_col[:CHUNK], g_col[CHUNK:]) - g_row
            decay = jnp.where(causal, jnp.exp(jnp.where(causal, g_diff, 0.0)), 0.0)
            kb = kh * b_col
            kq = _mm_nt(jnp.concatenate([kb, qh], axis=0), kh)
            lows.append(jnp.where(strict, side_by_side(kq[:pair]) * decay, 0.0))
            a_s[rows, head_cols(h)] = block_diag((side_by_side(kq[pair:]) * decay).astype(BF16))
            rhs.append(jnp.concatenate([vh * b_col, kb * eg_col], axis=1).astype(BF16))
            qd_s[rows, head_cols(h)] = (qh * eg_col).astype(BF16)
            kt_s[rows, head_cols(h)] = (kh * e_tail[:, la:la + 1]).astype(BF16)
            yield
        invs = [eye - jnp.where(xor_ij == 1, low, 0.0) for low in lows]
        level = 1
        while (1 << level) < CHUNK:
            joins = (xor_ij >> level) == 1
            cs = [block_diag(jnp.where(joins, low, 0.0).astype(BF16)) for low in lows]
            xs = [inv.astype(BF16) for inv in invs]
            xcs = []
            for x, c in zip(xs, cs):
                xcs.append(_mm(x, c))
                yield
            for i, (xc, x) in enumerate(zip(xcs, xs)):
                invs[i] = invs[i] - _mm(xc, block_diag(x))
                yield
            level += 1
        for (r0, h), t_inv, r in zip(problems, invs, rhs):
            rows = slice(r0, r0 + pair)
            uw = _mm(block_diag(t_inv.astype(BF16)), r)
            u_s[rows, head_cols(h)] = uw[:, :DN_HEAD_DIM]
            w_s[rows, head_cols(h)] = uw[:, DN_HEAD_DIM:].astype(BF16)
            yield

    levels = CHUNK.bit_length() - 2
    factor_steps = (2 + 2 * levels) * pairs_per_sub * DN_HEADS

    def recur(j):
        for p in range(pairs_per_sub):
            p0 = j * sub + p * pair
            v_new = [[None, None] for _ in heads]
            q_state = [[None, None] for _ in heads]
            for cc in range(2):
                r0 = p0 + cc * CHUNK
                rows = slice(r0, r0 + CHUNK)
                e_end = jnp.exp(gc_s[r0 + CHUNK - 1:r0 + CHUNK, :])
                s_in = [state[h] for h in heads]
                prods = []
                for h in heads:
                    lhs = jnp.concatenate([w_s[rows, head_cols(h)], qd_s[rows, head_cols(h)]], axis=0)
                    prods.append(_mm(lhs, s_in[h]))
                for h in heads:
                    v_new[h][cc] = u_s[rows, head_cols(h)] - prods[h][:CHUNK]
                    q_state[h][cc] = prods[h][CHUNK:]
                yield
                for h in heads:
                    la = DN_HEADS + h
                    state[h] = (s_in[h] * e_end[:, la:la + 1]
                                + _mm_tn(kt_s[rows, head_cols(h)], v_new[h][cc]))
                yield
            rows = slice(p0, p0 + pair)
            for h in heads:
                o = (jnp.concatenate(q_state[h], axis=0)
                     + _mm(a_s[rows, head_cols(h)], jnp.concatenate(v_new[h], axis=0)))
                o = o * lax.rsqrt(jnp.mean(o * o, axis=-1, keepdims=True) + NORM_EPS) * normw_ref[...]
                y_ref[rows, head_cols(h)] = (o * _silu(z_s[rows, head_cols(h)])).astype(y_ref.dtype)
                yield

    recur_steps = (4 + DN_HEADS) * pairs_per_sub

    stages = ((front, front_steps), (factor, factor_steps), (recur, recur_steps))
    for slot in range(n_sub + len(stages) - 1):
        _interleave([(stage(slot - k), steps) for k, (stage, steps) in enumerate(stages)
                     if 0 <= slot - k < n_sub])


def _w_all_spec(name, rows):
    off, width = _W_ALL_LAYOUT[name]
    return pl.BlockSpec((rows, width), lambda *_: (0, off // width))


def _deltanet_branch(x, w_all, conv_w, alog_row, dtb_row, normw_row):
    b, s, d = x.shape
    tile = DN_TILE
    const = lambda bi, ti: (0, 0)
    return pl.pallas_call(
        _dn_kernel,
        grid=(b, s // tile),
        in_specs=[
            pl.BlockSpec((None, tile, d), lambda bi, ti: (bi, ti, 0)),
            _w_all_spec("dn_qkv", d),
            _w_all_spec("dn_z", d),
            _w_all_spec("dn_ba", d),
            pl.BlockSpec(conv_w.shape, const),
            pl.BlockSpec(alog_row.shape, const),
            pl.BlockSpec(dtb_row.shape, const),
            pl.BlockSpec(normw_row.shape, const),
        ],
        out_specs=pl.BlockSpec((None, tile, DN_WIDTH), lambda bi, ti: (bi, ti, 0)),
        out_shape=jax.ShapeDtypeStruct((b, s, DN_WIDTH), BF16),
        scratch_shapes=[
            pltpu.VMEM((tile + SUBLANES, 3 * DN_WIDTH), F32),
            pltpu.VMEM((tile, DN_WIDTH), F32),
            pltpu.VMEM((tile, DN_WIDTH), F32),
            pltpu.VMEM((tile, DN_WIDTH), F32),
            pltpu.VMEM((tile, DN_WIDTH), F32),
            pltpu.VMEM((tile, LANES), F32),
            pltpu.VMEM((tile, LANES), F32),
            pltpu.VMEM((LANES, tile), F32),
            pltpu.VMEM((tile, DN_WIDTH), F32),
            pltpu.VMEM((tile, DN_WIDTH), BF16),
            pltpu.VMEM((tile, DN_WIDTH), BF16),
            pltpu.VMEM((tile, DN_WIDTH), BF16),
            pltpu.VMEM((tile, DN_WIDTH), BF16),
            pltpu.VMEM((DN_HEADS, DN_HEAD_DIM, DN_HEAD_DIM), F32),
        ],
        compiler_params=pltpu.CompilerParams(
            dimension_semantics=("arbitrary", "arbitrary"), vmem_limit_bytes=VMEM_LIMIT_BYTES),
        name="deltanet_branch",
    )(x, w_all, w_all, w_all, conv_w, alog_row, dtb_row, normw_row)


def _swa_kernel(sinks_ref, x_ref, wq_ref, wkv_ref, wz_ref, rope_ref, y_ref, kband, vband, q_s, z_s):
    t = pl.program_id(1)
    w = WINDOW
    tile = x_ref.shape[0]

    @pl.when(t == 0)
    def _():
        kband[:, 0:w, :] = jnp.zeros((2 * SWA_KV_HEADS, w, LANES), BF16)
        vband[:, 0:w, :] = jnp.zeros((2 * SWA_KV_HEADS, w, LANES), BF16)

    half = ROPE_DIM // 2
    sub = SWA_SUB
    n_sub = tile // sub
    n_pairs = SWA_Q_HEADS // 2
    group = SWA_Q_HEADS // SWA_KV_HEADS
    lo = lax.broadcasted_iota(jnp.int32, (sub, LANES), 1) < SWA_HEAD_DIM

    qi = lax.broadcasted_iota(jnp.int32, (w, 2 * w), 0)
    kj = lax.broadcasted_iota(jnp.int32, (w, 2 * w), 1)
    in_band = (kj > qi) & (kj <= qi + w)
    bias = jnp.where(in_band, 0.0, MASK_VALUE).astype(F32)
    bias_first = jnp.where(in_band & (kj >= jnp.where(t == 0, w, 0)), 0.0, MASK_VALUE).astype(F32)

    def front(j):
        r0 = j * sub
        rows = slice(r0, r0 + sub)
        cos_p = rope_ref[rows, 0:LANES]
        sin_a = rope_ref[rows, LANES:2 * LANES]
        sin_b = rope_ref[rows, 2 * LANES:3 * LANES]

        def rope(v):
            return v * cos_p + pltpu.roll(v, LANES - half, 1) * sin_a + pltpu.roll(v, half, 1) * sin_b

        xb = x_ref[rows, :].astype(BF16)
        q = jnp.dot(xb, wq_ref[...], preferred_element_type=F32)
        yield
        for pair in range(n_pairs):
            p0 = pair * LANES
            q_s[rows, p0:p0 + LANES] = rope(q[:, p0:p0 + LANES]).astype(BF16)
        yield
        kv = jnp.dot(xb, wkv_ref[...], preferred_element_type=F32)
        z_s[rows, :] = jnp.dot(xb, wz_ref[...], preferred_element_type=F32)
        yield
        k = rope(kv[:, :LANES])
        v = kv[:, LANES:]
        band_rows = slice(w + r0, w + r0 + sub)
        for src, band in ((k, kband), (v, vband)):
            swapped = pltpu.roll(src, SWA_HEAD_DIM, 1)
            band[0, band_rows, :] = jnp.where(lo, src, 0.0).astype(BF16)
            band[1, band_rows, :] = jnp.where(lo, 0.0, swapped).astype(BF16)
            band[2, band_rows, :] = jnp.where(lo, swapped, 0.0).astype(BF16)
            band[3, band_rows, :] = jnp.where(lo, 0.0, src).astype(BF16)
        yield

    front_steps = 4

    slabs_per_group = group // 2
    upper_rows = lax.broadcasted_iota(jnp.int32, (slabs_per_group * w, 1), 0) >= w

    def kv_group(r0, g, blk_bias):
        slabs = [slice((slabs_per_group * g + i) * LANES, (slabs_per_group * g + i + 1) * LANES)
                 for i in range(slabs_per_group)]
        q_rows = jnp.concatenate([q_s[r0:r0 + w, cols] for cols in slabs], axis=0)
        bias_rows = jnp.concatenate([blk_bias] * slabs_per_group, axis=0)
        scores, sinks = [], []
        for hf in range(2):
            heads_hf = [group * g + 2 * i + hf for i in range(slabs_per_group)]
            sinks.append(jnp.where(upper_rows, sinks_ref[heads_hf[1]], sinks_ref[heads_hf[0]]))
            scores.append(_mm_nt(q_rows, kband[2 * g + hf, r0:r0 + 2 * w, :]) + bias_rows)
        yield
        probs, denoms = [], []
        for s, sink in zip(scores, sinks):
            m = jnp.maximum(jnp.max(jnp.maximum(s[:, :w], s[:, w:]), axis=-1, keepdims=True), sink)
            p = jnp.exp(s - m)
            denoms.append(jnp.sum(p[:, :w] + p[:, w:], axis=-1, keepdims=True) + jnp.exp(sink - m))
            probs.append(p.astype(BF16))
        yield
        acc = None
        for hf in range(2):
            o = _mm(probs[hf], vband[2 * g + hf, r0:r0 + 2 * w, :]) / denoms[hf]
            acc = o if acc is None else acc + o
        for i, cols in enumerate(slabs):
            y_ref[r0:r0 + w, cols] = (
                acc[i * w:(i + 1) * w] * _silu(z_s[r0:r0 + w, cols])).astype(y_ref.dtype)
        yield

    def attend(j):
        items = [(j * sub + b * w, g) for b in range(sub // w) for g in range(SWA_KV_HEADS)]
        for g0 in range(0, len(items), SWA_GROUPS_IN_FLIGHT):
            streams = [kv_group(r0, g, bias_first if r0 == 0 else bias)
                       for r0, g in items[g0:g0 + SWA_GROUPS_IN_FLIGHT]]
            for _ in range(3):
                for stream in streams:
                    next(stream)
                yield

    attend_steps = 3 * (sub // w) * SWA_KV_HEADS // SWA_GROUPS_IN_FLIGHT

    stages = ((front, front_steps), (attend, attend_steps))
    for slot in range(n_sub + len(stages) - 1):
        _interleave([(stage(slot - k), steps) for k, (stage, steps) in enumerate(stages)
                     if 0 <= slot - k < n_sub])

    kband[:, 0:w, :] = kband[:, tile:tile + w, :]
    vband[:, 0:w, :] = vband[:, tile:tile + w, :]


def _swa_branch(x, sinks, w_all, rope_tab):
    b, s, d = x.shape
    w = WINDOW
    tile = SWA_TILE
    return pl.pallas_call(
        _swa_kernel,
        grid=(b, s // tile),
        in_specs=[
            pl.BlockSpec(memory_space=pltpu.SMEM),
            pl.BlockSpec((None, tile, d), lambda bi, ti: (bi, ti, 0)),
            _w_all_spec("swa_q", d),
            _w_all_spec("swa_kv", d),
            _w_all_spec("swa_z", d),
            pl.BlockSpec((tile, 3 * LANES), lambda bi, ti: (ti, 0)),
        ],
        out_specs=pl.BlockSpec((None, tile, SWA_WIDTH), lambda bi, ti: (bi, ti, 0)),
        out_shape=jax.ShapeDtypeStruct((b, s, SWA_WIDTH), BF16),
        scratch_shapes=[
            pltpu.VMEM((2 * SWA_KV_HEADS, w + tile, LANES), BF16),
            pltpu.VMEM((2 * SWA_KV_HEADS, w + tile, LANES), BF16),
            pltpu.VMEM((tile, SWA_WIDTH), BF16),
            pltpu.VMEM((tile, SWA_WIDTH), F32),
        ],
        compiler_params=pltpu.CompilerParams(
            dimension_semantics=("arbitrary", "arbitrary"), vmem_limit_bytes=VMEM_LIMIT_BYTES),
        name="swa_branch",
    )(sinks, x, w_all, w_all, w_all, rope_tab)


def _rope_table(seq):
    half = ROPE_DIM // 2
    inv_freq = ROPE_THETA ** (-jnp.arange(0, ROPE_DIM, 2, dtype=F32) / ROPE_DIM)
    ang = jnp.arange(seq, dtype=F32)[:, None] * inv_freq[None, :]
    cos, sin = jnp.cos(ang), jnp.sin(ang)
    rest = SWA_HEAD_DIM - ROPE_DIM
    ones = jnp.ones((seq, rest), F32)
    zeros = jnp.zeros((seq, rest), F32)
    zh = jnp.zeros((seq, half), F32)
    cos_p = jnp.concatenate([cos, cos, ones], axis=1)
    sin_a = jnp.concatenate([-sin, zh, zeros], axis=1)
    sin_b = jnp.concatenate([zh, sin, zeros], axis=1)
    reps = LANES // SWA_HEAD_DIM
    return jnp.concatenate([jnp.tile(cos_p, (1, reps)), jnp.tile(sin_a, (1, reps)),
                            jnp.tile(sin_b, (1, reps))], axis=1)


def _merge_kernel(alpha, x_ref, ya_ref, yb_ref, wg_ref, wa_ref, wb_ref, wo_ref, lng_ref, lnb_ref, o_ref,
                  merged_s):
    tile = x_ref.shape[0]
    sub = MERGE_SUB
    n_sub = tile // sub

    def gated_merge(j):
        rows = slice(j * sub, (j + 1) * sub)
        xb = x_ref[rows, :].astype(BF16)
        gate_a = jax.nn.sigmoid(jnp.dot(xb, wg_ref[:, :D_MODEL], preferred_element_type=F32))
        pa = jnp.dot(ya_ref[rows, :], wa_ref[...], preferred_element_type=F32)
        yield
        gate_b = jax.nn.sigmoid(jnp.dot(xb, wg_ref[:, D_MODEL:], preferred_element_type=F32))
        pb = jnp.dot(yb_ref[rows, :], wb_ref[...], preferred_element_type=F32)
        yield
        merged_s[rows, :] = (gate_a * pa + gate_b * pb).astype(BF16)
        yield

    def project_norm(j):
        rows = slice(j * sub, (j + 1) * sub)
        out = jnp.dot(merged_s[rows, :], wo_ref[...], preferred_element_type=F32)
        yield
        r = alpha * x_ref[rows, :] + out
        mu = jnp.mean(r, axis=-1, keepdims=True)
        cen = r - mu
        var = jnp.mean(cen * cen, axis=-1, keepdims=True)
        o_ref[rows, :] = (cen * lax.rsqrt(var + LN_EPS) * lng_ref[...] + lnb_ref[...]).astype(o_ref.dtype)
        yield

    stages = ((gated_merge, 3), (project_norm, 2))
    for slot in range(n_sub + len(stages) - 1):
        _interleave([(stage(slot - k), steps) for k, (stage, steps) in enumerate(stages)
                     if 0 <= slot - k < n_sub])


def _merge(x2, ya2, yb2, w_all, wa, wb, wo, lng_row, lnb_row, alpha):
    n, d = x2.shape
    tile = MERGE_TILE
    const = lambda i: (0, 0)
    row = lambda i: (i, 0)
    return pl.pallas_call(
        functools.partial(_merge_kernel, alpha),
        grid=(n // tile,),
        in_specs=[
            pl.BlockSpec((tile, d), row),
            pl.BlockSpec((tile, ya2.shape[1]), row),
            pl.BlockSpec((tile, yb2.shape[1]), row),
            _w_all_spec("gates", d),
            pl.BlockSpec(wa.shape, const),
            pl.BlockSpec(wb.shape, const),
            pl.BlockSpec(wo.shape, const),
            pl.BlockSpec(lng_row.shape, const),
            pl.BlockSpec(lnb_row.shape, const),
        ],
        out_specs=pl.BlockSpec((tile, d), row),
        out_shape=jax.ShapeDtypeStruct((n, d), x2.dtype),
        scratch_shapes=[pltpu.VMEM((tile, d), BF16)],
        compiler_params=pltpu.CompilerParams(
            dimension_semantics=("arbitrary",), vmem_limit_bytes=VMEM_LIMIT_BYTES),
        name="merge_out_norm",
    )(x2, ya2, yb2, w_all, wa, wb, wo, lng_row, lnb_row)


def _lane_row(vec, offset):
    n = vec.shape[0]
    return jnp.pad(vec.astype(F32), (offset, LANES - offset - n)).reshape(1, LANES)


def _pack_w_in(w_in):
    o = _OFF_SWA
    groups = {
        "dn_qkv": w_in[:, _OFF_QKV:_OFF_DN_Z],
        "dn_z": w_in[:, _OFF_DN_Z:_OFF_DN_BA],
        "gates": w_in[:, _OFF_GATE:],
        "swa_q": w_in[:, o:o + SWA_WIDTH] * (SWA_HEAD_DIM ** -0.5),
        "swa_z": w_in[:, o + SWA_WIDTH + 2 * SWA_KV_WIDTH:_OFF_GATE],
        "swa_kv": w_in[:, o + SWA_WIDTH:o + SWA_WIDTH + 2 * SWA_KV_WIDTH],
        "dn_ba": jnp.pad(w_in[:, _OFF_DN_BA:_OFF_SWA], ((0, 0), (0, LANES - 2 * DN_HEADS))),
    }
    assert list(groups) == list(_W_ALL_LAYOUT)
    return jnp.concatenate([g.astype(BF16) for g in groups.values()], axis=1)


def _layer(x, w_in, conv_w, a_log, dt_bias, dn_norm_w, sinks, w_branch, w_out, ln_g, ln_b, rope_tab, alpha):
    b, s, d = x.shape
    w_all = _pack_w_in(w_in)
    y_a = _deltanet_branch(x, w_all, conv_w.astype(F32), _lane_row(a_log, DN_HEADS),
                           _lane_row(dt_bias, DN_HEADS), dn_norm_w.astype(F32).reshape(1, DN_HEAD_DIM))
    y_b = _swa_branch(x, sinks.astype(F32), w_all, rope_tab)
    w_ab = w_branch.astype(BF16)
    out = _merge(x.reshape(b * s, d), y_a.reshape(b * s, DN_WIDTH), y_b.reshape(b * s, SWA_WIDTH),
                 w_all, w_ab[0], w_ab[1], w_out.astype(BF16),
                 ln_g.astype(F32).reshape(1, d), ln_b.astype(F32).reshape(1, d), alpha)
    return out.reshape(b, s, d)


def kernel(x, w_in, conv_w, a_log, dt_bias, dn_norm_w, sinks, w_branch, w_out, ln_g, ln_b):
    depth = w_in.shape[0]
    alpha = (2.0 * depth) ** 0.25
    rope_tab = _rope_table(x.shape[1])
    for layer in range(depth):
        x = _layer(x, w_in[layer], conv_w[layer], a_log[layer], dt_bias[layer], dn_norm_w[layer],
                   sinks[layer], w_branch[layer], w_out[layer], ln_g[layer], ln_b[layer], rope_tab, alpha)
    return x
```

```python
import functools

import jax
import jax.numpy as jnp
from jax import lax
from jax.experimental import pallas as pl
from jax.experimental.pallas import tpu as pltpu

F32 = jnp.float32
BF16 = jnp.bfloat16

D_MODEL = 1024
DN_HEADS = 4
DN_HEAD_DIM = 128
DN_WIDTH = DN_HEADS * DN_HEAD_DIM
CONV_WIDTH = 4
CHUNK = 64
SWA_Q_HEADS = 8
SWA_KV_HEADS = 2
SWA_HEAD_DIM = 64
SWA_WIDTH = SWA_Q_HEADS * SWA_HEAD_DIM
SWA_KV_WIDTH = SWA_KV_HEADS * SWA_HEAD_DIM
WINDOW = 128
ROPE_THETA = 500000.0
ROPE_DIM = SWA_HEAD_DIM // 4
LN_EPS = 1e-5
NORM_EPS = 1e-6
MASK_VALUE = -1e30

LANES = 128
SUBLANES = 8
VMEM_LIMIT_BYTES = 48 * 1024 * 1024

DN_TILE = 1024
DN_SUB = 256
DN_PROJ_COLS = 256
SWA_TILE = 1024
SWA_SUB = 256
SWA_GROUPS_IN_FLIGHT = 2
MERGE_TILE = 1024
MERGE_SUB = 256
assert CONV_WIDTH == 4

_OFF_QKV = 0
_OFF_DN_Z = 3 * DN_WIDTH
_OFF_DN_BA = _OFF_DN_Z + DN_WIDTH
_OFF_SWA = _OFF_DN_BA + 2 * DN_HEADS
_SWA_COLS = SWA_WIDTH + 2 * SWA_KV_WIDTH + SWA_WIDTH
_OFF_GATE = _OFF_SWA + _SWA_COLS


def _make_w_all_layout():
    widths = (("dn_qkv", 3 * DN_WIDTH), ("dn_z", DN_WIDTH), ("gates", 2 * D_MODEL), ("swa_q", SWA_WIDTH),
              ("swa_z", SWA_WIDTH), ("swa_kv", 2 * SWA_KV_WIDTH), ("dn_ba", LANES))
    layout, off = {}, 0
    for name, width in widths:
        assert off % width == 0
        layout[name] = (off, width)
        off += width
    return layout


_W_ALL_LAYOUT = _make_w_all_layout()


def _mm(a, b):
    return jnp.dot(a.astype(BF16), b.astype(BF16), preferred_element_type=F32)


def _mm_nt(a, b):
    return lax.dot_general(a.astype(BF16), b.astype(BF16), (((1,), (1,)), ((), ())),
                           preferred_element_type=F32)


def _mm_tn(a, b):
    return lax.dot_general(a.astype(BF16), b.astype(BF16), (((0,), (0,)), ((), ())),
                           preferred_element_type=F32)


def _silu(v):
    return v * jax.nn.sigmoid(v)


def _interleave(streams):
    live = [[gen, 0, steps] for gen, steps in streams]
    while live:
        entry = min(live, key=lambda e: e[1] / e[2])
        try:
            next(entry[0])
            entry[1] += 1
        except StopIteration:
            live.remove(entry)


def _dn_kernel(x_ref, wqkv_ref, wz_ref, wba_ref, convw_ref, alog_ref, dtb_ref, normw_ref, y_ref,
               xb_s, hbuf, q_s, k_s, v_s, z_s, beta_s, gc_s, gct_s, u_s, w_s, qd_s, kt_s, a_s, state):
    t = pl.program_id(1)
    tile = x_ref.shape[0]
    sub = DN_SUB
    n_sub = tile // sub
    halo = SUBLANES
    pair = 2 * CHUNK
    pairs_per_sub = sub // pair
    heads = range(DN_HEADS)

    @pl.when(t == 0)
    def _():
        hbuf[0:halo, :] = jnp.zeros((halo, 3 * DN_WIDTH), F32)
        state[...] = jnp.zeros_like(state)

    row = lax.broadcasted_iota(jnp.int32, (CHUNK, pair), 0)
    lane = lax.broadcasted_iota(jnp.int32, (CHUNK, pair), 1)
    col = lane % CHUNK
    left = lane < CHUNK
    causal = row >= col
    strict = row > col
    xor_ij = row ^ col
    eye = jnp.where(row == col, 1.0, 0.0).astype(F32)
    first_chunk = lax.broadcasted_iota(jnp.int32, (pair, LANES), 0) < CHUNK
    pos = lax.broadcasted_iota(jnp.int32, (sub, LANES), 0) % CHUNK

    def head_cols(h):
        return slice(h * DN_HEAD_DIM, (h + 1) * DN_HEAD_DIM)

    def side_by_side(m):
        return jnp.where(left, m[:CHUNK], m[CHUNK:])

    def block_diag(m):
        zero = jnp.zeros_like(m)
        return jnp.concatenate([jnp.where(left, m, zero), jnp.where(left, zero, m)], axis=0)

    def front(j):
        r0 = j * sub
        rows = slice(r0, r0 + sub)
        xb_s[rows, :] = x_ref[rows, :].astype(BF16)
        ba = jnp.dot(xb_s[rows, :], wba_ref[...], preferred_element_type=F32)
        yield
        beta_s[rows, :] = jax.nn.sigmoid(ba)
        xg = ba + dtb_ref[...]
        softplus = jnp.maximum(xg, 0.0) + jnp.log1p(jnp.exp(-jnp.abs(xg)))
        gc = -jnp.exp(alog_ref[...]) * softplus
        step = 1
        while step < CHUNK:
            gc = gc + jnp.where(pos >= step, pltpu.roll(gc, step, 0), 0.0)
            step *= 2
        gc_s[rows, :] = gc
        gct_s[:, rows] = gc.T
        yield
        dests = (q_s, k_s, v_s)
        slabs_per_chunk = DN_PROJ_COLS // DN_HEAD_DIM
        for s in range(3 * DN_HEADS):
            if s % slabs_per_chunk == 0:
                chunk = slice(s * DN_HEAD_DIM, s * DN_HEAD_DIM + DN_PROJ_COLS)
                hbuf[halo + r0:halo + r0 + sub, chunk] = jnp.dot(
                    xb_s[rows, :], wqkv_ref[:, chunk], preferred_element_type=F32)
                z0 = (s // slabs_per_chunk) * DN_PROJ_COLS
                if z0 < DN_WIDTH:
                    z_s[rows, z0:z0 + DN_PROJ_COLS] = jnp.dot(
                        xb_s[rows, :], wz_ref[:, z0:z0 + DN_PROJ_COLS], preferred_element_type=F32)
                yield
            cols = slice(s * DN_HEAD_DIM, (s + 1) * DN_HEAD_DIM)
            ext = hbuf[r0:r0 + halo + sub, cols]
            prev = pltpu.roll(ext, 1, 0)
            older = convw_ref[0:1, cols] * prev + convw_ref[1:2, cols] * ext
            newer = convw_ref[2:3, cols] * prev + convw_ref[3:4, cols] * ext
            acc = (pltpu.roll(older, 2, 0) + newer)[halo:, :]
            yield
            a = _silu(acc)
            if s < 2 * DN_HEADS:
                a = a * lax.rsqrt(jnp.sum(a * a, axis=-1, keepdims=True) + NORM_EPS)
                if s < DN_HEADS:
                    a = a * (DN_HEAD_DIM ** -0.5)
            dests[s // DN_HEADS][rows, head_cols(s % DN_HEADS)] = a
            yield
        if j == n_sub - 1:
            hbuf[0:halo, :] = hbuf[tile:tile + halo, :]

    front_steps = 2 + 3 * DN_WIDTH // DN_PROJ_COLS + 2 * 3 * DN_HEADS

    def factor(j):
        problems = [(j * sub + p * pair, h) for p in range(pairs_per_sub) for h in heads]
        lows, rhs = [], []
        for r0, h in problems:
            rows = slice(r0, r0 + pair)
            la = DN_HEADS + h
            if h == 0:
                gc_p = gc_s[rows, :]
                e_gc = jnp.exp(gc_p)
                g_end = jnp.where(first_chunk, gc_p[CHUNK - 1:CHUNK, :], gc_p[pair - 1:pair, :])
                e_tail = jnp.exp(g_end - gc_p)
                beta_p = beta_s[rows, :]
            g_col = gc_p[:, la:la + 1]
            g_row = gct_s[la:la + 1, rows]
            b_col = beta_p[:, h:h + 1]
            eg_col = e_gc[:, la:la + 1]
            qh = q_s[rows, head_cols(h)]
            kh = k_s[rows, head_cols(h)]
            vh = v_s[rows, head_cols(h)]
            g_diff = jnp.where(left, g_col[:CHUNK], g_col[CHUNK:]) - g_row
            decay = jnp.where(causal, jnp.exp(jnp.where(causal, g_diff, 0.0)), 0.0)
            kb = kh * b_col
            kq = _mm_nt(jnp.concatenate([kb, qh], axis=0), kh)
            lows.append(jnp.where(strict, side_by_side(kq[:pair]) * decay, 0.0))
            a_s[rows, head_cols(h)] = block_diag((side_by_side(kq[pair:]) * decay).astype(BF16))
            rhs.append(jnp.concatenate([vh * b_col, kb * eg_col], axis=1).astype(BF16))
            qd_s[rows, head_cols(h)] = (qh * eg_col).astype(BF16)
            kt_s[rows, head_cols(h)] = (kh * e_tail[:, la:la + 1]).astype(BF16)
            yield
        invs = [eye - jnp.where(xor_ij == 1, low, 0.0) for low in lows]
        level = 1
        while (1 << level) < CHUNK:
            joins = (xor_ij >> level) == 1
            cs = [block_diag(jnp.where(joins, low, 0.0).astype(BF16)) for low in lows]
            xs = [inv.astype(BF16) for inv in invs]
            xcs = []
            for x, c in zip(xs, cs):
                xcs.append(_mm(x, c))
                yield
            for i, (xc, x) in enumerate(zip(xcs, xs)):
                invs[i] = invs[i] - _mm(xc, block_diag(x))
                yield
            level += 1
        for (r0, h), t_inv, r in zip(problems, invs, rhs):
            rows = slice(r0, r0 + pair)
            uw = _mm(block_diag(t_inv.astype(BF16)), r)
            u_s[rows, head_cols(h)] = uw[:, :DN_HEAD_DIM]
            w_s[rows, head_cols(h)] = uw[:, DN_HEAD_DIM:].astype(BF16)
            yield

    levels = CHUNK.bit_length() - 2
    factor_steps = (2 + 2 * levels) * pairs_per_sub * DN_HEADS

    def recur(j):
        for p in range(pairs_per_sub):
            p0 = j * sub + p * pair
            v_new = [[None, None] for _ in heads]
            q_state = [[None, None] for _ in heads]
            for cc in range(2):
                r0 = p0 + cc * CHUNK
                rows = slice(r0, r0 + CHUNK)
                e_end = jnp.exp(gc_s[r0 + CHUNK - 1:r0 + CHUNK, :])
                s_in = [state[h] for h in heads]
                prods = []
                for h in heads:
                    lhs = jnp.concatenate([w_s[rows, head_cols(h)], qd_s[rows, head_cols(h)]], axis=0)
                    prods.append(_mm(lhs, s_in[h]))
                for h in heads:
                    v_new[h][cc] = u_s[rows, head_cols(h)] - prods[h][:CHUNK]
                    q_state[h][cc] = prods[h][CHUNK:]
                yield
                for h in heads:
                    la = DN_HEADS + h
                    state[h] = (s_in[h] * e_end[:, la:la + 1]
                                + _mm_tn(kt_s[rows, head_cols(h)], v_new[h][cc]))
                yield
            rows = slice(p0, p0 + pair)
            for h in heads:
                o = (jnp.concatenate(q_state[h], axis=0)
                     + _mm(a_s[rows, head_cols(h)], jnp.concatenate(v_new[h], axis=0)))
                o = o * lax.rsqrt(jnp.mean(o * o, axis=-1, keepdims=True) + NORM_EPS) * normw_ref[...]
                y_ref[rows, head_cols(h)] = (o * _silu(z_s[rows, head_cols(h)])).astype(y_ref.dtype)
                yield

    recur_steps = (4 + DN_HEADS) * pairs_per_sub

    stages = ((front, front_steps), (factor, factor_steps), (recur, recur_steps))
    for slot in range(n_sub + len(stages) - 1):
        _interleave([(stage(slot - k), steps) for k, (stage, steps) in enumerate(stages)
                     if 0 <= slot - k < n_sub])


def _w_all_spec(name, rows):
    off, width = _W_ALL_LAYOUT[name]
    return pl.BlockSpec((rows, width), lambda *_: (0, off // width))


def _deltanet_branch(x, w_all, conv_w, alog_row, dtb_row, normw_row):
    b, s, d = x.shape
    tile = DN_TILE
    const = lambda bi, ti: (0, 0)
    return pl.pallas_call(
        _dn_kernel,
        grid=(b, s // tile),
        in_specs=[
            pl.BlockSpec((None, tile, d), lambda bi, ti: (bi, ti, 0)),
            _w_all_spec("dn_qkv", d),
            _w_all_spec("dn_z", d),
            _w_all_spec("dn_ba", d),
            pl.BlockSpec(conv_w.shape, const),
            pl.BlockSpec(alog_row.shape, const),
            pl.BlockSpec(dtb_row.shape, const),
            pl.BlockSpec(normw_row.shape, const),
        ],
        out_specs=pl.BlockSpec((None, tile, DN_WIDTH), lambda bi, ti: (bi, ti, 0)),
        out_shape=jax.ShapeDtypeStruct((b, s, DN_WIDTH), BF16),
        scratch_shapes=[
            pltpu.VMEM((tile, d), BF16),
            pltpu.VMEM((tile + SUBLANES, 3 * DN_WIDTH), F32),
            pltpu.VMEM((tile, DN_WIDTH), F32),
            pltpu.VMEM((tile, DN_WIDTH), F32),
            pltpu.VMEM((tile, DN_WIDTH), F32),
            pltpu.VMEM((tile, DN_WIDTH), F32),
            pltpu.VMEM((tile, LANES), F32),
            pltpu.VMEM((tile, LANES), F32),
            pltpu.VMEM((LANES, tile), F32),
            pltpu.VMEM((tile, DN_WIDTH), F32),
            pltpu.VMEM((tile, DN_WIDTH), BF16),
            pltpu.VMEM((tile, DN_WIDTH), BF16),
            pltpu.VMEM((tile, DN_WIDTH), BF16),
            pltpu.VMEM((tile, DN_WIDTH), BF16),
            pltpu.VMEM((DN_HEADS, DN_HEAD_DIM, DN_HEAD_DIM), F32),
        ],
        compiler_params=pltpu.CompilerParams(
            dimension_semantics=("arbitrary", "arbitrary"), vmem_limit_bytes=VMEM_LIMIT_BYTES),
        name="deltanet_branch",
    )(x, w_all, w_all, w_all, conv_w, alog_row, dtb_row, normw_row)


def _swa_kernel(sinks_ref, x_ref, wq_ref, wkv_ref, wz_ref, rope_ref, y_ref, kband, vband, q_s, z_s):
    t = pl.program_id(1)
    w = WINDOW
    tile = x_ref.shape[0]

    @pl.when(t == 0)
    def _():
        kband[:, 0:w, :] = jnp.zeros((2 * SWA_KV_HEADS, w, LANES), BF16)
        vband[:, 0:w, :] = jnp.zeros((2 * SWA_KV_HEADS, w, LANES), BF16)

    half = ROPE_DIM // 2
    sub = SWA_SUB
    n_sub = tile // sub
    n_pairs = SWA_Q_HEADS // 2
    group = SWA_Q_HEADS // SWA_KV_HEADS
    lo = lax.broadcasted_iota(jnp.int32, (sub, LANES), 1) < SWA_HEAD_DIM

    qi = lax.broadcasted_iota(jnp.int32, (w, 2 * w), 0)
    kj = lax.broadcasted_iota(jnp.int32, (w, 2 * w), 1)
    in_band = (kj > qi) & (kj <= qi + w)
    bias = jnp.where(in_band, 0.0, MASK_VALUE).astype(F32)
    bias_first = jnp.where(in_band & (kj >= jnp.where(t == 0, w, 0)), 0.0, MASK_VALUE).astype(F32)

    def front(j):
        r0 = j * sub
        rows = slice(r0, r0 + sub)
        cos_p = rope_ref[rows, 0:LANES]
        sin_a = rope_ref[rows, LANES:2 * LANES]
        sin_b = rope_ref[rows, 2 * LANES:3 * LANES]

        def rope(v):
            return v * cos_p + pltpu.roll(v, LANES - half, 1) * sin_a + pltpu.roll(v, half, 1) * sin_b

        xb = x_ref[rows, :].astype(BF16)
        q = jnp.dot(xb, wq_ref[...], preferred_element_type=F32)
        yield
        for pair in range(n_pairs):
            p0 = pair * LANES
            q_s[rows, p0:p0 + LANES] = rope(q[:, p0:p0 + LANES]).astype(BF16)
        yield
        kv = jnp.dot(xb, wkv_ref[...], preferred_element_type=F32)
        z_s[rows, :] = jnp.dot(xb, wz_ref[...], preferred_element_type=F32)
        yield
        k = rope(kv[:, :LANES])
        v = kv[:, LANES:]
        band_rows = slice(w + r0, w + r0 + sub)
        for src, band in ((k, kband), (v, vband)):
            swapped = pltpu.roll(src, SWA_HEAD_DIM, 1)
            band[0, band_rows, :] = jnp.where(lo, src, 0.0).astype(BF16)
            band[1, band_rows, :] = jnp.where(lo, 0.0, swapped).astype(BF16)
            band[2, band_rows, :] = jnp.where(lo, swapped, 0.0).astype(BF16)
            band[3, band_rows, :] = jnp.where(lo, 0.0, src).astype(BF16)
        yield

    front_steps = 4

    slabs_per_group = group // 2
    upper_rows = lax.broadcasted_iota(jnp.int32, (slabs_per_group * w, 1), 0) >= w

    def kv_group(r0, g, blk_bias):
        slabs = [slice((slabs_per_group * g + i) * LANES, (slabs_per_group * g + i + 1) * LANES)
                 for i in range(slabs_per_group)]
        q_rows = jnp.concatenate([q_s[r0:r0 + w, cols] for cols in slabs], axis=0)
        bias_rows = jnp.concatenate([blk_bias] * slabs_per_group, axis=0)
        scores, sinks = [], []
        for hf in range(2):
            heads_hf = [group * g + 2 * i + hf for i in range(slabs_per_group)]
            sinks.append(jnp.where(upper_rows, sinks_ref[heads_hf[1]], sinks_ref[heads_hf[0]]))
            scores.append(_mm_nt(q_rows, kband[2 * g + hf, r0:r0 + 2 * w, :]) + bias_rows)
        yield
        probs, denoms = [], []
        for s, sink in zip(scores, sinks):
            m = jnp.maximum(jnp.max(jnp.maximum(s[:, :w], s[:, w:]), axis=-1, keepdims=True), sink)
            p = jnp.exp(s - m)
            denoms.append(jnp.sum(p[:, :w] + p[:, w:], axis=-1, keepdims=True) + jnp.exp(sink - m))
            probs.append(p.astype(BF16))
        yield
        acc = None
        for hf in range(2):
            o = _mm(probs[hf], vband[2 * g + hf, r0:r0 + 2 * w, :]) / denoms[hf]
            acc = o if acc is None else acc + o
        for i, cols in enumerate(slabs):
            y_ref[r0:r0 + w, cols] = (
                acc[i * w:(i + 1) * w] * _silu(z_s[r0:r0 + w, cols])).astype(y_ref.dtype)
        yield

    def attend(j):
        items = [(j * sub + b * w, g) for b in range(sub // w) for g in range(SWA_KV_HEADS)]
        for g0 in range(0, len(items), SWA_GROUPS_IN_FLIGHT):
            streams = [kv_group(r0, g, bias_first if r0 == 0 else bias)
                       for r0, g in items[g0:g0 + SWA_GROUPS_IN_FLIGHT]]
            for _ in range(3):
                for stream in streams:
                    next(stream)
                yield

    attend_steps = 3 * (sub // w) * SWA_KV_HEADS // SWA_GROUPS_IN_FLIGHT

    stages = ((front, front_steps), (attend, attend_steps))
    for slot in range(n_sub + len(stages) - 1):
        _interleave([(stage(slot - k), steps) for k, (stage, steps) in enumerate(stages)
                     if 0 <= slot - k < n_sub])

    kband[:, 0:w, :] = kband[:, tile:tile + w, :]
    vband[:, 0:w, :] = vband[:, tile:tile + w, :]


def _swa_branch(x, sinks, w_all, rope_tab):
    b, s, d = x.shape
    w = WINDOW
    tile = SWA_TILE
    return pl.pallas_call(
        _swa_kernel,
        grid=(b, s // tile),
        in_specs=[
            pl.BlockSpec(memory_space=pltpu.SMEM),
            pl.BlockSpec((None, tile, d), lambda bi, ti: (bi, ti, 0)),
            _w_all_spec("swa_q", d),
            _w_all_spec("swa_kv", d),
            _w_all_spec("swa_z", d),
            pl.BlockSpec((tile, 3 * LANES), lambda bi, ti: (ti, 0)),
        ],
        out_specs=pl.BlockSpec((None, tile, SWA_WIDTH), lambda bi, ti: (bi, ti, 0)),
        out_shape=jax.ShapeDtypeStruct((b, s, SWA_WIDTH), BF16),
        scratch_shapes=[
            pltpu.VMEM((2 * SWA_KV_HEADS, w + tile, LANES), BF16),
            pltpu.VMEM((2 * SWA_KV_HEADS, w + tile, LANES), BF16),
            pltpu.VMEM((tile, SWA_WIDTH), BF16),
            pltpu.VMEM((tile, SWA_WIDTH), F32),
        ],
        compiler_params=pltpu.CompilerParams(
            dimension_semantics=("arbitrary", "arbitrary"), vmem_limit_bytes=VMEM_LIMIT_BYTES),
        name="swa_branch",
    )(sinks, x, w_all, w_all, w_all, rope_tab)


def _rope_table(seq):
    half = ROPE_DIM // 2
    inv_freq = ROPE_THETA ** (-jnp.arange(0, ROPE_DIM, 2, dtype=F32) / ROPE_DIM)
    ang = jnp.arange(seq, dtype=F32)[:, None] * inv_freq[None, :]
    cos, sin = jnp.cos(ang), jnp.sin(ang)
    rest = SWA_HEAD_DIM - ROPE_DIM
    ones = jnp.ones((seq, rest), F32)
    zeros = jnp.zeros((seq, rest), F32)
    zh = jnp.zeros((seq, half), F32)
    cos_p = jnp.concatenate([cos, cos, ones], axis=1)
    sin_a = jnp.concatenate([-sin, zh, zeros], axis=1)
    sin_b = jnp.concatenate([zh, sin, zeros], axis=1)
    reps = LANES // SWA_HEAD_DIM
    return jnp.concatenate([jnp.tile(cos_p, (1, reps)), jnp.tile(sin_a, (1, reps)),
                            jnp.tile(sin_b, (1, reps))], axis=1)


def _merge_kernel(alpha, x_ref, ya_ref, yb_ref, wg_ref, wa_ref, wb_ref, wo_ref, lng_ref, lnb_ref, o_ref,
                  merged_s):
    tile = x_ref.shape[0]
    sub = MERGE_SUB
    n_sub = tile // sub

    def gated_merge(j):
        rows = slice(j * sub, (j + 1) * sub)
        xb = x_ref[rows, :].astype(BF16)
        gate_a = jax.nn.sigmoid(jnp.dot(xb, wg_ref[:, :D_MODEL], preferred_element_type=F32))
        pa = jnp.dot(ya_ref[rows, :], wa_ref[...], preferred_element_type=F32)
        yield
        gate_b = jax.nn.sigmoid(jnp.dot(xb, wg_ref[:, D_MODEL:], preferred_element_type=F32))
        pb = jnp.dot(yb_ref[rows, :], wb_ref[...], preferred_element_type=F32)
        yield
        merged_s[rows, :] = (gate_a * pa + gate_b * pb).astype(BF16)
        yield

    def project_norm(j):
        rows = slice(j * sub, (j + 1) * sub)
        out = jnp.dot(merged_s[rows, :], wo_ref[...], preferred_element_type=F32)
        yield
        r = alpha * x_ref[rows, :] + out
        mu = jnp.mean(r, axis=-1, keepdims=True)
        cen = r - mu
        var = jnp.mean(cen * cen, axis=-1, keepdims=True)
        o_ref[rows, :] = (cen * lax.rsqrt(var + LN_EPS) * lng_ref[...] + lnb_ref[...]).astype(o_ref.dtype)
        yield

    stages = ((gated_merge, 3), (project_norm, 2))
    for slot in range(n_sub + len(stages) - 1):
        _interleave([(stage(slot - k), steps) for k, (stage, steps) in enumerate(stages)
                     if 0 <= slot - k < n_sub])


def _merge(x2, ya2, yb2, w_all, wa, wb, wo, lng_row, lnb_row, alpha):
    n, d = x2.shape
    tile = MERGE_TILE
    const = lambda i: (0, 0)
    row = lambda i: (i, 0)
    return pl.pallas_call(
        functools.partial(_merge_kernel, alpha),
        grid=(n // tile,),
        in_specs=[
            pl.BlockSpec((tile, d), row),
            pl.BlockSpec((tile, ya2.shape[1]), row),
            pl.BlockSpec((tile, yb2.shape[1]), row),
            _w_all_spec("gates", d),
            pl.BlockSpec(wa.shape, const),
            pl.BlockSpec(wb.shape, const),
            pl.BlockSpec(wo.shape, const),
            pl.BlockSpec(lng_row.shape, const),
            pl.BlockSpec(lnb_row.shape, const),
        ],
        out_specs=pl.BlockSpec((tile, d), row),
        out_shape=jax.ShapeDtypeStruct((n, d), x2.dtype),
        scratch_shapes=[pltpu.VMEM((tile, d), BF16)],
        compiler_params=pltpu.CompilerParams(
            dimension_semantics=("arbitrary",), vmem_limit_bytes=VMEM_LIMIT_BYTES),
        name="merge_out_norm",
    )(x2, ya2, yb2, w_all, wa, wb, wo, lng_row, lnb_row)


def _lane_row(vec, offset):
    n = vec.shape[0]
    return jnp.pad(vec.astype(F32), (offset, LANES - offset - n)).reshape(1, LANES)


def _pack_w_in(w_in):
    o = _OFF_SWA
    groups = {
        "dn_qkv": w_in[:, _OFF_QKV:_OFF_DN_Z],
        "dn_z": w_in[:, _OFF_DN_Z:_OFF_DN_BA],
        "gates": w_in[:, _OFF_GATE:],
        "swa_q": w_in[:, o:o + SWA_WIDTH] * (SWA_HEAD_DIM ** -0.5),
        "swa_z": w_in[:, o + SWA_WIDTH + 2 * SWA_KV_WIDTH:_OFF_GATE],
        "swa_kv": w_in[:, o + SWA_WIDTH:o + SWA_WIDTH + 2 * SWA_KV_WIDTH],
        "dn_ba": jnp.pad(w_in[:, _OFF_DN_BA:_OFF_SWA], ((0, 0), (0, LANES - 2 * DN_HEADS))),
    }
    assert list(groups) == list(_W_ALL_LAYOUT)
    return jnp.concatenate([g.astype(BF16) for g in groups.values()], axis=1)


def _layer(x, w_in, conv_w, a_log, dt_bias, dn_norm_w, sinks, w_branch, w_out, ln_g, ln_b, rope_tab, alpha):
    b, s, d = x.shape
    w_all = _pack_w_in(w_in)
    y_a = _deltanet_branch(x, w_all, conv_w.astype(F32), _lane_row(a_log, DN_HEADS),
                           _lane_row(dt_bias, DN_HEADS), dn_norm_w.astype(F32).reshape(1, DN_HEAD_DIM))
    y_b = _swa_branch(x, sinks.astype(F32), w_all, rope_tab)
    w_ab = w_branch.astype(BF16)
    out = _merge(x.reshape(b * s, d), y_a.reshape(b * s, DN_WIDTH), y_b.reshape(b * s, SWA_WIDTH),
                 w_all, w_ab[0], w_ab[1], w_out.astype(BF16),
                 ln_g.astype(F32).reshape(1, d), ln_b.astype(F32).reshape(1, d), alpha)
    return out.reshape(b, s, d)


def kernel(x, w_in, conv_w, a_log, dt_bias, dn_norm_w, sinks, w_branch, w_out, ln_g, ln_b):
    depth = w_in.shape[0]
    alpha = (2.0 * depth) ** 0.25
    rope_tab = _rope_table(x.shape[1])
    for layer in range(depth):
        x = _layer(x, w_in[layer], conv_w[layer], a_log[layer], dt_bias[layer], dn_norm_w[layer],
                   sinks[layer], w_branch[layer], w_out[layer], ln_g[layer], ln_b[layer], rope_tab, alpha)
    return x
```

```python
import functools

import jax
import jax.numpy as jnp
from jax import lax
from jax.experimental import pallas as pl
from jax.experimental.pallas import tpu as pltpu

F32 = jnp.float32
BF16 = jnp.bfloat16

D_MODEL = 1024
DN_HEADS = 4
DN_HEAD_DIM = 128
DN_WIDTH = DN_HEADS * DN_HEAD_DIM
CONV_WIDTH = 4
CHUNK = 64
SWA_Q_HEADS = 8
SWA_KV_HEADS = 2
SWA_HEAD_DIM = 64
SWA_WIDTH = SWA_Q_HEADS * SWA_HEAD_DIM
SWA_KV_WIDTH = SWA_KV_HEADS * SWA_HEAD_DIM
WINDOW = 128
ROPE_THETA = 500000.0
ROPE_DIM = SWA_HEAD_DIM // 4
LN_EPS = 1e-5
NORM_EPS = 1e-6
MASK_VALUE = -1e30

LANES = 128
SUBLANES = 8
VMEM_LIMIT_BYTES = 48 * 1024 * 1024

DN_TILE = 1024
DN_SUB = 256
DN_PROJ_COLS = 256
SWA_TILE = 1024
SWA_SUB = 256
SWA_GROUPS_IN_FLIGHT = 2
MERGE_TILE = 1024
MERGE_SUB = 256
assert CONV_WIDTH == 4

_OFF_QKV = 0
_OFF_DN_Z = 3 * DN_WIDTH
_OFF_DN_BA = _OFF_DN_Z + DN_WIDTH
_OFF_SWA = _OFF_DN_BA + 2 * DN_HEADS
_SWA_COLS = SWA_WIDTH + 2 * SWA_KV_WIDTH + SWA_WIDTH
_OFF_GATE = _OFF_SWA + _SWA_COLS


def _make_w_all_layout():
    widths = (("dn_qkv", 3 * DN_WIDTH), ("dn_z", DN_WIDTH), ("gates", 2 * D_MODEL), ("swa_q", SWA_WIDTH),
              ("swa_z", SWA_WIDTH), ("swa_kv", 2 * SWA_KV_WIDTH), ("dn_ba", LANES))
    layout, off = {}, 0
    for name, width in widths:
        assert off % width == 0
        layout[name] = (off, width)
        off += width
    return layout


_W_ALL_LAYOUT = _make_w_all_layout()


def _mm(a, b):
    return jnp.dot(a.astype(BF16), b.astype(BF16), preferred_element_type=F32)


def _mm_nt(a, b):
    return lax.dot_general(a.astype(BF16), b.astype(BF16), (((1,), (1,)), ((), ())),
                           preferred_element_type=F32)


def _mm_tn(a, b):
    return lax.dot_general(a.astype(BF16), b.astype(BF16), (((0,), (0,)), ((), ())),
                           preferred_element_type=F32)


def _silu(v):
    return v * jax.nn.sigmoid(v)


def _interleave(streams):
    live = [[gen, 0, steps] for gen, steps in streams]
    while live:
        entry = min(live, key=lambda e: e[1] / e[2])
        try:
            next(entry[0])
            entry[1] += 1
        except StopIteration:
            live.remove(entry)


def _dn_kernel(x_ref, wqkv_ref, wz_ref, wba_ref, convw_ref, alog_ref, dtb_ref, normw_ref, y_ref,
               xb_s, hbuf, q_s, k_s, v_s, z_s, beta_s, gc_s, gct_s, u_s, w_s, qd_s, kt_s, a_s, state):
    t = pl.program_id(1)
    tile = x_ref.shape[0]
    sub = DN_SUB
    n_sub = tile // sub
    halo = SUBLANES
    pair = 2 * CHUNK
    pairs_per_sub = sub // pair
    heads = range(DN_HEADS)

    @pl.when(t == 0)
    def _():
        hbuf[0:halo, :] = jnp.zeros((halo, 3 * DN_WIDTH), F32)
        state[...] = jnp.zeros_like(state)

    row = lax.broadcasted_iota(jnp.int32, (CHUNK, pair), 0)
    lane = lax.broadcasted_iota(jnp.int32, (CHUNK, pair), 1)
    col = lane % CHUNK
    left = lane < CHUNK
    causal = row >= col
    strict = row > col
    xor_ij = row ^ col
    eye = jnp.where(row == col, 1.0, 0.0).astype(F32)
    first_chunk = lax.broadcasted_iota(jnp.int32, (pair, LANES), 0) < CHUNK
    pos = lax.broadcasted_iota(jnp.int32, (sub, LANES), 0) % CHUNK

    def head_cols(h):
        return slice(h * DN_HEAD_DIM, (h + 1) * DN_HEAD_DIM)

    def side_by_side(m):
        return jnp.where(left, m[:CHUNK], m[CHUNK:])

    def block_diag(m):
        zero = jnp.zeros_like(m)
        return jnp.concatenate([jnp.where(left, m, zero), jnp.where(left, zero, m)], axis=0)

    def front(j):
        r0 = j * sub
        rows = slice(r0, r0 + sub)
        xb_s[rows, :] = x_ref[rows, :].astype(BF16)
        ba = jnp.dot(xb_s[rows, :], wba_ref[...], preferred_element_type=F32)
        yield
        beta_s[rows, :] = jax.nn.sigmoid(ba)
        xg = ba + dtb_ref[...]
        softplus = jnp.maximum(xg, 0.0) + jnp.log1p(jnp.exp(-jnp.abs(xg)))
        gc = -jnp.exp(alog_ref[...]) * softplus
        step = 1
        while step < CHUNK:
            gc = gc + jnp.where(pos >= step, pltpu.roll(gc, step, 0), 0.0)
            step *= 2
        gc_s[rows, :] = gc
        gct_s[:, rows] = gc.T
        yield
        dests = (q_s, k_s, v_s)
        slabs_per_chunk = DN_PROJ_COLS // DN_HEAD_DIM
        for s in range(3 * DN_HEADS):
            if s % slabs_per_chunk == 0:
                chunk = slice(s * DN_HEAD_DIM, s * DN_HEAD_DIM + DN_PROJ_COLS)
                hbuf[halo + r0:halo + r0 + sub, chunk] = jnp.dot(
                    xb_s[rows, :], wqkv_ref[:, chunk], preferred_element_type=F32)
                z0 = (s // slabs_per_chunk) * DN_PROJ_COLS
                if z0 < DN_WIDTH:
                    z_s[rows, z0:z0 + DN_PROJ_COLS] = jnp.dot(
                        xb_s[rows, :], wz_ref[:, z0:z0 + DN_PROJ_COLS], preferred_element_type=F32)
                yield
            cols = slice(s * DN_HEAD_DIM, (s + 1) * DN_HEAD_DIM)
            ext = hbuf[r0:r0 + halo + sub, cols]
            prev = pltpu.roll(ext, 1, 0)
            older = convw_ref[0:1, cols] * prev + convw_ref[1:2, cols] * ext
            newer = convw_ref[2:3, cols] * prev + convw_ref[3:4, cols] * ext
            acc = (pltpu.roll(older, 2, 0) + newer)[halo:, :]
            yield
            a = _silu(acc)
            if s < 2 * DN_HEADS:
                a = a * lax.rsqrt(jnp.sum(a * a, axis=-1, keepdims=True) + NORM_EPS)
                if s < DN_HEADS:
                    a = a * (DN_HEAD_DIM ** -0.5)
            dests[s // DN_HEADS][rows, head_cols(s % DN_HEADS)] = a
            yield
        if j == n_sub - 1:
            hbuf[0:halo, :] = hbuf[tile:tile + halo, :]

    front_steps = 2 + 3 * DN_WIDTH // DN_PROJ_COLS + 2 * 3 * DN_HEADS

    def factor(j):
        problems = [(j * sub + p * pair, h) for p in range(pairs_per_sub) for h in heads]
        lows, rhs = [], []
        for r0, h in problems:
            rows = slice(r0, r0 + pair)
            la = DN_HEADS + h
            if h == 0:
                gc_p = gc_s[rows, :]
                e_gc = jnp.exp(gc_p)
                g_end = jnp.where(first_chunk, gc_p[CHUNK - 1:CHUNK, :], gc_p[pair - 1:pair, :])
                e_tail = jnp.exp(g_end - gc_p)
                beta_p = beta_s[rows, :]
            g_col = gc_p[:, la:la + 1]
            g_row = gct_s[la:la + 1, rows]
            b_col = beta_p[:, h:h + 1]
            eg_col = e_gc[:, la:la + 1]
            qh = q_s[rows, head_cols(h)]
            kh = k_s[rows, head_cols(h)]
            vh = v_s[rows, head_cols(h)]
            g_diff = jnp.where(left, g_col[:CHUNK], g_col[CHUNK:]) - g_row
            decay = jnp.where(causal, jnp.exp(jnp.where(causal, g_diff, 0.0)), 0.0)
            kb = kh * b_col
            kq = _mm_nt(jnp.concatenate([kb, qh], axis=0), kh)
            lows.append(jnp.where(strict, side_by_side(kq[:pair]) * decay, 0.0))
            a_s[rows, head_cols(h)] = block_diag((side_by_side(kq[pair:]) * decay).astype(BF16))
            rhs.append(jnp.concatenate([vh * b_col, kb * eg_col], axis=1).astype(BF16))
            qd_s[rows, head_cols(h)] = (qh * eg_col).astype(BF16)
            kt_s[rows, head_cols(h)] = (kh * e_tail[:, la:la + 1]).astype(BF16)
            yield
        invs = [eye - jnp.where(xor_ij == 1, low, 0.0) for low in lows]
        level = 1
        while (1 << level) < CHUNK:
            joins = (xor_ij >> level) == 1
            cs = [block_diag(jnp.where(joins, low, 0.0).astype(BF16)) for low in lows]
            xs = [inv.astype(BF16) for inv in invs]
            xcs = []
            for x, c in zip(xs, cs):
                xcs.append(_mm(x, c))
                yield
            for i, (xc, x) in enumerate(zip(xcs, xs)):
                invs[i] = invs[i] - _mm(xc, block_diag(x))
                yield
            level += 1
        for (r0, h), t_inv, r in zip(problems, invs, rhs):
            rows = slice(r0, r0 + pair)
            uw = _mm(block_diag(t_inv.astype(BF16)), r)
            u_s[rows, head_cols(h)] = uw[:, :DN_HEAD_DIM]
            w_s[rows, head_cols(h)] = uw[:, DN_HEAD_DIM:].astype(BF16)
            yield

    levels = CHUNK.bit_length() - 2
    factor_steps = (2 + 2 * levels) * pairs_per_sub * DN_HEADS

    def recur(j):
        for p in range(pairs_per_sub):
            p0 = j * sub + p * pair
            v_new = [[None, None] for _ in heads]
            q_state = [[None, None] for _ in heads]
            for cc in range(2):
                r0 = p0 + cc * CHUNK
                rows = slice(r0, r0 + CHUNK)
                e_end = jnp.exp(gc_s[r0 + CHUNK - 1:r0 + CHUNK, :])
                s_in = [state[h] for h in heads]
                prods = []
                for h in heads:
                    lhs = jnp.concatenate([w_s[rows, head_cols(h)], qd_s[rows, head_cols(h)]], axis=0)
                    prods.append(_mm(lhs, s_in[h]))
                for h in heads:
                    v_new[h][cc] = u_s[rows, head_cols(h)] - prods[h][:CHUNK]
                    q_state[h][cc] = prods[h][CHUNK:]
                yield
                for h in heads:
                    la = DN_HEADS + h
                    state[h] = (s_in[h] * e_end[:, la:la + 1]
                                + _mm_tn(kt_s[rows, head_cols(h)], v_new[h][cc]))
                yield
            rows = slice(p0, p0 + pair)
            for h in heads:
                o = (jnp.concatenate(q_state[h], axis=0)
                     + _mm(a_s[rows, head_cols(h)], jnp.concatenate(v_new[h], axis=0)))
                o = o * lax.rsqrt(jnp.mean(o * o, axis=-1, keepdims=True) + NORM_EPS) * normw_ref[...]
                y_ref[rows, head_cols(h)] = (o * _silu(z_s[rows, head_cols(h)])).astype(y_ref.dtype)
                yield

    recur_steps = (4 + DN_HEADS) * pairs_per_sub

    stages = ((front, front_steps), (factor, factor_steps), (recur, recur_steps))
    for slot in range(n_sub + len(stages) - 1):
        _interleave([(stage(slot - k), steps) for k, (stage, steps) in enumerate(stages)
                     if 0 <= slot - k < n_sub])


def _w_all_spec(name, rows):
    off, width = _W_ALL_LAYOUT[name]
    return pl.BlockSpec((rows, width), lambda *_: (0, off // width))


def _deltanet_branch(x, w_all, conv_w, alog_row, dtb_row, normw_row):
    b, s, d = x.shape
    tile = DN_TILE
    const = lambda bi, ti: (0, 0)
    return pl.pallas_call(
        _dn_kernel,
        grid=(b, s // tile),
        in_specs=[
            pl.BlockSpec((None, tile, d), lambda bi, ti: (bi, ti, 0)),
            _w_all_spec("dn_qkv", d),
            _w_all_spec("dn_z", d),
            _w_all_spec("dn_ba", d),
            pl.BlockSpec(conv_w.shape, const),
            pl.BlockSpec(alog_row.shape, const),
            pl.BlockSpec(dtb_row.shape, const),
            pl.BlockSpec(normw_row.shape, const),
        ],
        out_specs=pl.BlockSpec((None, tile, DN_WIDTH), lambda bi, ti: (bi, ti, 0)),
        out_shape=jax.ShapeDtypeStruct((b, s, DN_WIDTH), BF16),
        scratch_shapes=[
            pltpu.VMEM((tile, d), BF16),
            pltpu.VMEM((tile + SUBLANES, 3 * DN_WIDTH), F32),
            pltpu.VMEM((tile, DN_WIDTH), F32),
            pltpu.VMEM((tile, DN_WIDTH), F32),
            pltpu.VMEM((tile, DN_WIDTH), F32),
            pltpu.VMEM((tile, DN_WIDTH), F32),
            pltpu.VMEM((tile, LANES), F32),
            pltpu.VMEM((tile, LANES), F32),
            pltpu.VMEM((LANES, tile), F32),
            pltpu.VMEM((tile, DN_WIDTH), F32),
            pltpu.VMEM((tile, DN_WIDTH), BF16),
            pltpu.VMEM((tile, DN_WIDTH), BF16),
            pltpu.VMEM((tile, DN_WIDTH), BF16),
            pltpu.VMEM((tile, DN_WIDTH), BF16),
            pltpu.VMEM((DN_HEADS, DN_HEAD_DIM, DN_HEAD_DIM), F32),
        ],
        compiler_params=pltpu.CompilerParams(
            dimension_semantics=("arbitrary", "arbitrary"), vmem_limit_bytes=VMEM_LIMIT_BYTES),
        name="deltanet_branch",
    )(x, w_all, w_all, w_all, conv_w, alog_row, dtb_row, normw_row)


def _swa_kernel(sinks_ref, x_ref, wq_ref, wkv_ref, wz_ref, rope_off_ref, rope_start_ref, y_ref,
                kband, vband, q_s, z_s):
    t = pl.program_id(1)
    w = WINDOW
    tile = x_ref.shape[0]

    @pl.when(t == 0)
    def _():
        kband[:, 0:w, :] = jnp.zeros((2 * SWA_KV_HEADS, w, LANES), BF16)
        vband[:, 0:w, :] = jnp.zeros((2 * SWA_KV_HEADS, w, LANES), BF16)

    half = ROPE_DIM // 2
    sub = SWA_SUB
    n_sub = tile // sub
    n_pairs = SWA_Q_HEADS // 2
    group = SWA_Q_HEADS // SWA_KV_HEADS
    lane = lax.broadcasted_iota(jnp.int32, (sub, LANES), 1)
    lo = lane < SWA_HEAD_DIM
    head_lane = lane % SWA_HEAD_DIM

    qi = lax.broadcasted_iota(jnp.int32, (w, 2 * w), 0)
    kj = lax.broadcasted_iota(jnp.int32, (w, 2 * w), 1)
    in_band = (kj > qi) & (kj <= qi + w)
    bias = jnp.where(in_band, 0.0, MASK_VALUE).astype(F32)
    bias_first = jnp.where(in_band & (kj >= jnp.where(t == 0, w, 0)), 0.0, MASK_VALUE).astype(F32)

    def front(j):
        r0 = j * sub
        rows = slice(r0, r0 + sub)
        cos_o = rope_off_ref[rows, 0:LANES]
        sin_o = rope_off_ref[rows, LANES:2 * LANES]
        cos_s = rope_start_ref[:, 0:LANES]
        sin_s = rope_start_ref[:, LANES:2 * LANES]
        cos_p = cos_s * cos_o - sin_s * sin_o
        sin_p = sin_s * cos_o + cos_s * sin_o
        sin_a = jnp.where(head_lane < half, -sin_p, 0.0)
        sin_b = jnp.where(head_lane >= half, sin_p, 0.0)

        def rope(v):
            return v * cos_p + pltpu.roll(v, LANES - half, 1) * sin_a + pltpu.roll(v, half, 1) * sin_b

        xb = x_ref[rows, :].astype(BF16)
        q = jnp.dot(xb, wq_ref[...], preferred_element_type=F32)
        yield
        for pair in range(n_pairs):
            p0 = pair * LANES
            q_s[rows, p0:p0 + LANES] = rope(q[:, p0:p0 + LANES]).astype(BF16)
        yield
        kv = jnp.dot(xb, wkv_ref[...], preferred_element_type=F32)
        z_s[rows, :] = jnp.dot(xb, wz_ref[...], preferred_element_type=F32)
        yield
        k = rope(kv[:, :LANES])
        v = kv[:, LANES:]
        band_rows = slice(w + r0, w + r0 + sub)
        for src, band in ((k, kband), (v, vband)):
            swapped = pltpu.roll(src, SWA_HEAD_DIM, 1)
            band[0, band_rows, :] = jnp.where(lo, src, 0.0).astype(BF16)
            band[1, band_rows, :] = jnp.where(lo, 0.0, swapped).astype(BF16)
            band[2, band_rows, :] = jnp.where(lo, swapped, 0.0).astype(BF16)
            band[3, band_rows, :] = jnp.where(lo, 0.0, src).astype(BF16)
        yield

    front_steps = 4

    slabs_per_group = group // 2
    upper_rows = lax.broadcasted_iota(jnp.int32, (slabs_per_group * w, 1), 0) >= w

    def kv_group(r0, g, blk_bias):
        slabs = [slice((slabs_per_group * g + i) * LANES, (slabs_per_group * g + i + 1) * LANES)
                 for i in range(slabs_per_group)]
        q_rows = jnp.concatenate([q_s[r0:r0 + w, cols] for cols in slabs], axis=0)
        bias_rows = jnp.concatenate([blk_bias] * slabs_per_group, axis=0)
        scores, sinks = [], []
        for hf in range(2):
            heads_hf = [group * g + 2 * i + hf for i in range(slabs_per_group)]
            sinks.append(jnp.where(upper_rows, sinks_ref[heads_hf[1]], sinks_ref[heads_hf[0]]))
            scores.append(_mm_nt(q_rows, kband[2 * g + hf, r0:r0 + 2 * w, :]) + bias_rows)
        yield
        probs, denoms = [], []
        for s, sink in zip(scores, sinks):
            m = jnp.maximum(jnp.max(jnp.maximum(s[:, :w], s[:, w:]), axis=-1, keepdims=True), sink)
            p = jnp.exp(s - m)
            denoms.append(jnp.sum(p[:, :w] + p[:, w:], axis=-1, keepdims=True) + jnp.exp(sink - m))
            probs.append(p.astype(BF16))
        yield
        acc = None
        for hf in range(2):
            o = _mm(probs[hf], vband[2 * g + hf, r0:r0 + 2 * w, :]) / denoms[hf]
            acc = o if acc is None else acc + o
        for i, cols in enumerate(slabs):
            y_ref[r0:r0 + w, cols] = (
                acc[i * w:(i + 1) * w] * _silu(z_s[r0:r0 + w, cols])).astype(y_ref.dtype)
        yield

    def attend(j):
        items = [(j * sub + b * w, g) for b in range(sub // w) for g in range(SWA_KV_HEADS)]
        for g0 in range(0, len(items), SWA_GROUPS_IN_FLIGHT):
            streams = [kv_group(r0, g, bias_first if r0 == 0 else bias)
                       for r0, g in items[g0:g0 + SWA_GROUPS_IN_FLIGHT]]
            for _ in range(3):
                for stream in streams:
                    next(stream)
                yield

    attend_steps = 3 * (sub // w) * SWA_KV_HEADS // SWA_GROUPS_IN_FLIGHT

    stages = ((front, front_steps), (attend, attend_steps))
    for slot in range(n_sub + len(stages) - 1):
        _interleave([(stage(slot - k), steps) for k, (stage, steps) in enumerate(stages)
                     if 0 <= slot - k < n_sub])

    kband[:, 0:w, :] = kband[:, tile:tile + w, :]
    vband[:, 0:w, :] = vband[:, tile:tile + w, :]


def _swa_branch(x, sinks, w_all):
    b, s, d = x.shape
    w = WINDOW
    tile = SWA_TILE
    rope_offsets, rope_starts = _rope_tables(s, tile)
    return pl.pallas_call(
        _swa_kernel,
        grid=(b, s // tile),
        in_specs=[
            pl.BlockSpec(memory_space=pltpu.SMEM),
            pl.BlockSpec((None, tile, d), lambda bi, ti: (bi, ti, 0)),
            _w_all_spec("swa_q", d),
            _w_all_spec("swa_kv", d),
            _w_all_spec("swa_z", d),
            pl.BlockSpec((tile, 2 * LANES), lambda bi, ti: (0, 0)),
            pl.BlockSpec((None, 1, 2 * LANES), lambda bi, ti: (ti, 0, 0)),
        ],
        out_specs=pl.BlockSpec((None, tile, SWA_WIDTH), lambda bi, ti: (bi, ti, 0)),
        out_shape=jax.ShapeDtypeStruct((b, s, SWA_WIDTH), BF16),
        scratch_shapes=[
            pltpu.VMEM((2 * SWA_KV_HEADS, w + tile, LANES), BF16),
            pltpu.VMEM((2 * SWA_KV_HEADS, w + tile, LANES), BF16),
            pltpu.VMEM((tile, SWA_WIDTH), BF16),
            pltpu.VMEM((tile, SWA_WIDTH), F32),
        ],
        compiler_params=pltpu.CompilerParams(
            dimension_semantics=("arbitrary", "arbitrary"), vmem_limit_bytes=VMEM_LIMIT_BYTES),
        name="swa_branch",
    )(sinks, x, w_all, w_all, w_all, rope_offsets, rope_starts)


def _rope_tables(seq, tile):
    inv_freq = ROPE_THETA ** (-jnp.arange(0, ROPE_DIM, 2, dtype=F32) / ROPE_DIM)

    def lane_pattern(positions):
        ang = positions[:, None] * inv_freq[None, :]
        n = positions.shape[0]
        rest = SWA_HEAD_DIM - ROPE_DIM
        reps = LANES // SWA_HEAD_DIM
        cos = jnp.concatenate([jnp.cos(ang)] * 2 + [jnp.ones((n, rest), F32)], axis=1)
        sin = jnp.concatenate([jnp.sin(ang)] * 2 + [jnp.zeros((n, rest), F32)], axis=1)
        return jnp.concatenate([jnp.tile(cos, (1, reps)), jnp.tile(sin, (1, reps))], axis=1)

    offsets = lane_pattern(jnp.arange(tile, dtype=F32))
    starts = lane_pattern(jnp.arange(seq // tile, dtype=F32) * tile)
    return offsets, starts.reshape(seq // tile, 1, 2 * LANES)


def _merge_kernel(alpha, x_ref, ya_ref, yb_ref, wg_ref, wa_ref, wb_ref, wo_ref, lng_ref, lnb_ref, o_ref,
                  merged_s):
    tile = x_ref.shape[0]
    sub = MERGE_SUB
    n_sub = tile // sub

    def gated_merge(j):
        rows = slice(j * sub, (j + 1) * sub)
        xb = x_ref[rows, :].astype(BF16)
        gate_a = jax.nn.sigmoid(jnp.dot(xb, wg_ref[:, :D_MODEL], preferred_element_type=F32))
        pa = jnp.dot(ya_ref[rows, :], wa_ref[...], preferred_element_type=F32)
        yield
        gate_b = jax.nn.sigmoid(jnp.dot(xb, wg_ref[:, D_MODEL:], preferred_element_type=F32))
        pb = jnp.dot(yb_ref[rows, :], wb_ref[...], preferred_element_type=F32)
        yield
        merged_s[rows, :] = (gate_a * pa + gate_b * pb).astype(BF16)
        yield

    def project_norm(j):
        rows = slice(j * sub, (j + 1) * sub)
        out = jnp.dot(merged_s[rows, :], wo_ref[...], preferred_element_type=F32)
        yield
        r = alpha * x_ref[rows, :] + out
        mu = jnp.mean(r, axis=-1, keepdims=True)
        cen = r - mu
        var = jnp.mean(cen * cen, axis=-1, keepdims=True)
        o_ref[rows, :] = (cen * lax.rsqrt(var + LN_EPS) * lng_ref[...] + lnb_ref[...]).astype(o_ref.dtype)
        yield

    stages = ((gated_merge, 3), (project_norm, 2))
    for slot in range(n_sub + len(stages) - 1):
        _interleave([(stage(slot - k), steps) for k, (stage, steps) in enumerate(stages)
                     if 0 <= slot - k < n_sub])


def _merge(x2, ya2, yb2, w_all, wa, wb, wo, lng_row, lnb_row, alpha):
    n, d = x2.shape
    tile = MERGE_TILE
    const = lambda i: (0, 0)
    row = lambda i: (i, 0)
    return pl.pallas_call(
        functools.partial(_merge_kernel, alpha),
        grid=(n // tile,),
        in_specs=[
            pl.BlockSpec((tile, d), row),
            pl.BlockSpec((tile, ya2.shape[1]), row),
            pl.BlockSpec((tile, yb2.shape[1]), row),
            _w_all_spec("gates", d),
            pl.BlockSpec(wa.shape, const),
            pl.BlockSpec(wb.shape, const),
            pl.BlockSpec(wo.shape, const),
            pl.BlockSpec(lng_row.shape, const),
            pl.BlockSpec(lnb_row.shape, const),
        ],
        out_specs=pl.BlockSpec((tile, d), row),
        out_shape=jax.ShapeDtypeStruct((n, d), x2.dtype),
        scratch_shapes=[pltpu.VMEM((tile, d), BF16)],
        compiler_params=pltpu.CompilerParams(
            dimension_semantics=("arbitrary",), vmem_limit_bytes=VMEM_LIMIT_BYTES),
        name="merge_out_norm",
    )(x2, ya2, yb2, w_all, wa, wb, wo, lng_row, lnb_row)


def _lane_row(vec, offset):
    n = vec.shape[0]
    return jnp.pad(vec.astype(F32), (offset, LANES - offset - n)).reshape(1, LANES)


def _pack_w_in(w_in):
    o = _OFF_SWA
    groups = {
        "dn_qkv": w_in[:, _OFF_QKV:_OFF_DN_Z],
        "dn_z": w_in[:, _OFF_DN_Z:_OFF_DN_BA],
        "gates": w_in[:, _OFF_GATE:],
        "swa_q": w_in[:, o:o + SWA_WIDTH] * (SWA_HEAD_DIM ** -0.5),
        "swa_z": w_in[:, o + SWA_WIDTH + 2 * SWA_KV_WIDTH:_OFF_GATE],
        "swa_kv": w_in[:, o + SWA_WIDTH:o + SWA_WIDTH + 2 * SWA_KV_WIDTH],
        "dn_ba": jnp.pad(w_in[:, _OFF_DN_BA:_OFF_SWA], ((0, 0), (0, LANES - 2 * DN_HEADS))),
    }
    assert list(groups) == list(_W_ALL_LAYOUT)
    return jnp.concatenate([g.astype(BF16) for g in groups.values()], axis=1)


def _layer(x, w_in, conv_w, a_log, dt_bias, dn_norm_w, sinks, w_branch, w_out, ln_g, ln_b, alpha):
    b, s, d = x.shape
    w_all = _pack_w_in(w_in)
    y_a = _deltanet_branch(x, w_all, conv_w.astype(F32), _lane_row(a_log, DN_HEADS),
                           _lane_row(dt_bias, DN_HEADS), dn_norm_w.astype(F32).reshape(1, DN_HEAD_DIM))
    y_b = _swa_branch(x, sinks.astype(F32), w_all)
    w_ab = w_branch.astype(BF16)
    out = _merge(x.reshape(b * s, d), y_a.reshape(b * s, DN_WIDTH), y_b.reshape(b * s, SWA_WIDTH),
                 w_all, w_ab[0], w_ab[1], w_out.astype(BF16),
                 ln_g.astype(F32).reshape(1, d), ln_b.astype(F32).reshape(1, d), alpha)
    return out.reshape(b, s, d)


def kernel(x, w_in, conv_w, a_log, dt_bias, dn_norm_w, sinks, w_branch, w_out, ln_g, ln_b):
    depth = w_in.shape[0]
    alpha = (2.0 * depth) ** 0.25
    for layer in range(depth):
        x = _layer(x, w_in[layer], conv_w[layer], a_log[layer], dt_bias[layer], dn_norm_w[layer],
                   sinks[layer], w_branch[layer], w_out[layer], ln_g[layer], ln_b[layer], alpha)
    return x
```

```python
import functools

import jax
import jax.numpy as jnp
from jax import lax
from jax.experimental import pallas as pl
from jax.experimental.pallas import tpu as pltpu

F32 = jnp.float32
BF16 = jnp.bfloat16

D_MODEL = 1024
DN_HEADS = 4
DN_HEAD_DIM = 128
DN_WIDTH = DN_HEADS * DN_HEAD_DIM
CONV_WIDTH = 4
CHUNK = 64
SWA_Q_HEADS = 8
SWA_KV_HEADS = 2
SWA_HEAD_DIM = 64
SWA_WIDTH = SWA_Q_HEADS * SWA_HEAD_DIM
SWA_KV_WIDTH = SWA_KV_HEADS * SWA_HEAD_DIM
WINDOW = 128
ROPE_THETA = 500000.0
ROPE_DIM = SWA_HEAD_DIM // 4
LN_EPS = 1e-5
NORM_EPS = 1e-6
MASK_VALUE = -1e30

LANES = 128
SUBLANES = 8
VMEM_LIMIT_BYTES = 52 * 1024 * 1024

DN_TILE = 2048
DN_SUB = 256
DN_PROJ_COLS = 256
SWA_TILE = 2048
SWA_SUB = 256
SWA_GROUPS_IN_FLIGHT = 2
MERGE_TILE = 1024
MERGE_SUB = 256
assert CONV_WIDTH == 4

_OFF_QKV = 0
_OFF_DN_Z = 3 * DN_WIDTH
_OFF_DN_BA = _OFF_DN_Z + DN_WIDTH
_OFF_SWA = _OFF_DN_BA + 2 * DN_HEADS
_SWA_COLS = SWA_WIDTH + 2 * SWA_KV_WIDTH + SWA_WIDTH
_OFF_GATE = _OFF_SWA + _SWA_COLS


def _make_w_all_layout():
    widths = (("dn_qkv", 3 * DN_WIDTH), ("dn_z", DN_WIDTH), ("gates", 2 * D_MODEL), ("swa_q", SWA_WIDTH),
              ("swa_z", SWA_WIDTH), ("swa_kv", 2 * SWA_KV_WIDTH), ("dn_ba", LANES))
    layout, off = {}, 0
    for name, width in widths:
        assert off % width == 0
        layout[name] = (off, width)
        off += width
    return layout


_W_ALL_LAYOUT = _make_w_all_layout()


def _mm(a, b):
    return jnp.dot(a.astype(BF16), b.astype(BF16), preferred_element_type=F32)


def _mm_nt(a, b):
    return lax.dot_general(a.astype(BF16), b.astype(BF16), (((1,), (1,)), ((), ())),
                           preferred_element_type=F32)


def _mm_tn(a, b):
    return lax.dot_general(a.astype(BF16), b.astype(BF16), (((0,), (0,)), ((), ())),
                           preferred_element_type=F32)


def _silu(v):
    return v * jax.nn.sigmoid(v)


def _interleave(streams):
    live = [[gen, 0, steps] for gen, steps in streams]
    while live:
        entry = min(live, key=lambda e: e[1] / e[2])
        try:
            next(entry[0])
            entry[1] += 1
        except StopIteration:
            live.remove(entry)


def _dn_kernel(x_ref, wqkv_ref, wz_ref, wba_ref, convw_ref, alog_ref, dtb_ref, normw_ref, y_ref,
               xb_s, hbuf, q_s, k_s, v_s, z_s, beta_s, gc_s, gct_s, u_s, w_s, qd_s, kt_s, a_s, state):
    t = pl.program_id(1)
    tile = x_ref.shape[0]
    sub = DN_SUB
    n_sub = tile // sub
    halo = SUBLANES
    pair = 2 * CHUNK
    pairs_per_sub = sub // pair
    heads = range(DN_HEADS)

    @pl.when(t == 0)
    def _():
        hbuf[0:halo, :] = jnp.zeros((halo, 3 * DN_WIDTH), F32)
        state[...] = jnp.zeros_like(state)

    row = lax.broadcasted_iota(jnp.int32, (CHUNK, pair), 0)
    lane = lax.broadcasted_iota(jnp.int32, (CHUNK, pair), 1)
    col = lane % CHUNK
    left = lane < CHUNK
    causal = row >= col
    strict = row > col
    xor_ij = row ^ col
    eye = jnp.where(row == col, 1.0, 0.0).astype(F32)
    first_chunk = lax.broadcasted_iota(jnp.int32, (pair, LANES), 0) < CHUNK
    pos = lax.broadcasted_iota(jnp.int32, (sub, LANES), 0) % CHUNK

    def head_cols(h):
        return slice(h * DN_HEAD_DIM, (h + 1) * DN_HEAD_DIM)

    def ring_rows(ref, j, offset=0, size=sub, axis=0):
        start = (j % (ref.shape[axis] // sub)) * sub + offset
        return slice(start, start + size)

    def side_by_side(m):
        return jnp.where(left, m[:CHUNK], m[CHUNK:])

    def block_diag(m):
        zero = jnp.zeros_like(m)
        return jnp.concatenate([jnp.where(left, m, zero), jnp.where(left, zero, m)], axis=0)

    def front(j):
        r0 = j * sub
        rows = slice(r0, r0 + sub)
        xb_s[...] = x_ref[rows, :].astype(BF16)
        ba = jnp.dot(xb_s[...], wba_ref[...], preferred_element_type=F32)
        yield
        beta_s[ring_rows(beta_s, j), :] = jax.nn.sigmoid(ba)
        xg = ba + dtb_ref[...]
        softplus = jnp.maximum(xg, 0.0) + jnp.log1p(jnp.exp(-jnp.abs(xg)))
        gc = -jnp.exp(alog_ref[...]) * softplus
        step = 1
        while step < CHUNK:
            gc = gc + jnp.where(pos >= step, pltpu.roll(gc, step, 0), 0.0)
            step *= 2
        gc_s[ring_rows(gc_s, j), :] = gc
        gct_s[:, ring_rows(gct_s, j, axis=1)] = gc.T
        yield
        dests = (q_s, k_s, v_s)
        slabs_per_chunk = DN_PROJ_COLS // DN_HEAD_DIM
        for s in range(3 * DN_HEADS):
            if s % slabs_per_chunk == 0:
                chunk = slice(s * DN_HEAD_DIM, s * DN_HEAD_DIM + DN_PROJ_COLS)
                hbuf[halo:halo + sub, chunk] = jnp.dot(
                    xb_s[...], wqkv_ref[:, chunk], preferred_element_type=F32)
                z0 = (s // slabs_per_chunk) * DN_PROJ_COLS
                if z0 < DN_WIDTH:
                    z_s[ring_rows(z_s, j), z0:z0 + DN_PROJ_COLS] = jnp.dot(
                        xb_s[...], wz_ref[:, z0:z0 + DN_PROJ_COLS], preferred_element_type=F32)
                yield
            cols = slice(s * DN_HEAD_DIM, (s + 1) * DN_HEAD_DIM)
            ext = hbuf[:, cols]
            prev = pltpu.roll(ext, 1, 0)
            older = convw_ref[0:1, cols] * prev + convw_ref[1:2, cols] * ext
            newer = convw_ref[2:3, cols] * prev + convw_ref[3:4, cols] * ext
            acc = (pltpu.roll(older, 2, 0) + newer)[halo:, :]
            yield
            a = _silu(acc)
            if s < 2 * DN_HEADS:
                a = a * lax.rsqrt(jnp.sum(a * a, axis=-1, keepdims=True) + NORM_EPS)
                if s < DN_HEADS:
                    a = a * (DN_HEAD_DIM ** -0.5)
            dests[s // DN_HEADS][ring_rows(q_s, j), head_cols(s % DN_HEADS)] = a
            yield
        hbuf[0:halo, :] = hbuf[sub:sub + halo, :]

    front_steps = 2 + 3 * DN_WIDTH // DN_PROJ_COLS + 2 * 3 * DN_HEADS

    def factor(j):
        problems = [(p * pair, h) for p in range(pairs_per_sub) for h in heads]
        lows, rhs = [], []
        for r0, h in problems:
            la = DN_HEADS + h
            if h == 0:
                gc_p = gc_s[ring_rows(gc_s, j, r0, pair), :]
                e_gc = jnp.exp(gc_p)
                g_end = jnp.where(first_chunk, gc_p[CHUNK - 1:CHUNK, :], gc_p[pair - 1:pair, :])
                e_tail = jnp.exp(g_end - gc_p)
                beta_p = beta_s[ring_rows(beta_s, j, r0, pair), :]
            g_col = gc_p[:, la:la + 1]
            g_row = gct_s[la:la + 1, ring_rows(gct_s, j, r0, pair, axis=1)]
            b_col = beta_p[:, h:h + 1]
            eg_col = e_gc[:, la:la + 1]
            rows = ring_rows(q_s, j, r0, pair)
            qh = q_s[rows, head_cols(h)]
            kh = k_s[rows, head_cols(h)]
            vh = v_s[rows, head_cols(h)]
            rows = ring_rows(a_s, j, r0, pair)
            g_diff = jnp.where(left, g_col[:CHUNK], g_col[CHUNK:]) - g_row
            decay = jnp.where(causal, jnp.exp(jnp.where(causal, g_diff, 0.0)), 0.0)
            kb = kh * b_col
            kq = _mm_nt(jnp.concatenate([kb, qh], axis=0), kh)
            lows.append(jnp.where(strict, side_by_side(kq[:pair]) * decay, 0.0))
            a_s[rows, head_cols(h)] = block_diag((side_by_side(kq[pair:]) * decay).astype(BF16))
            rhs.append(jnp.concatenate([vh * b_col, kb * eg_col], axis=1).astype(BF16))
            qd_s[rows, head_cols(h)] = (qh * eg_col).astype(BF16)
            kt_s[rows, head_cols(h)] = (kh * e_tail[:, la:la + 1]).astype(BF16)
            yield
        invs = [eye - jnp.where(xor_ij == 1, low, 0.0) for low in lows]
        level = 1
        while (1 << level) < CHUNK:
            joins = (xor_ij >> level) == 1
            cs = [block_diag(jnp.where(joins, low, 0.0).astype(BF16)) for low in lows]
            xs = [inv.astype(BF16) for inv in invs]
            xcs = []
            for x, c in zip(xs, cs):
                xcs.append(_mm(x, c))
                yield
            for i, (xc, x) in enumerate(zip(xcs, xs)):
                invs[i] = invs[i] - _mm(xc, block_diag(x))
                yield
            level += 1
        for (r0, h), t_inv, r in zip(problems, invs, rhs):
            rows = ring_rows(u_s, j, r0, pair)
            uw = _mm(block_diag(t_inv.astype(BF16)), r)
            u_s[rows, head_cols(h)] = uw[:, :DN_HEAD_DIM]
            w_s[rows, head_cols(h)] = uw[:, DN_HEAD_DIM:].astype(BF16)
            yield

    levels = CHUNK.bit_length() - 2
    factor_steps = (2 + 2 * levels) * pairs_per_sub * DN_HEADS

    def recur(j):
        for p in range(pairs_per_sub):
            p0 = p * pair
            v_new = [[None, None] for _ in heads]
            q_state = [[None, None] for _ in heads]
            for cc in range(2):
                r0 = p0 + cc * CHUNK
                rows = ring_rows(u_s, j, r0, CHUNK)
                e_end = jnp.exp(gc_s[ring_rows(gc_s, j, r0 + CHUNK - 1, 1), :])
                s_in = [state[h] for h in heads]
                prods = []
                for h in heads:
                    lhs = jnp.concatenate([w_s[rows, head_cols(h)], qd_s[rows, head_cols(h)]], axis=0)
                    prods.append(_mm(lhs, s_in[h]))
                for h in heads:
                    v_new[h][cc] = u_s[rows, head_cols(h)] - prods[h][:CHUNK]
                    q_state[h][cc] = prods[h][CHUNK:]
                yield
                for h in heads:
                    la = DN_HEADS + h
                    state[h] = (s_in[h] * e_end[:, la:la + 1]
                                + _mm_tn(kt_s[rows, head_cols(h)], v_new[h][cc]))
                yield
            out_rows = slice(j * sub + p0, j * sub + p0 + pair)
            for h in heads:
                o = (jnp.concatenate(q_state[h], axis=0)
                     + _mm(a_s[ring_rows(a_s, j, p0, pair), head_cols(h)], jnp.concatenate(v_new[h], axis=0)))
                o = o * lax.rsqrt(jnp.mean(o * o, axis=-1, keepdims=True) + NORM_EPS) * normw_ref[...]
                gate = _silu(z_s[ring_rows(z_s, j, p0, pair), head_cols(h)])
                y_ref[out_rows, head_cols(h)] = (o * gate).astype(y_ref.dtype)
                yield

    recur_steps = (4 + DN_HEADS) * pairs_per_sub

    stages = ((front, front_steps), (factor, factor_steps), (recur, recur_steps))
    for slot in range(n_sub + len(stages) - 1):
        _interleave([(stage(slot - k), steps) for k, (stage, steps) in enumerate(stages)
                     if 0 <= slot - k < n_sub])


def _w_all_spec(name, rows):
    off, width = _W_ALL_LAYOUT[name]
    return pl.BlockSpec((rows, width), lambda *_: (0, off // width))


def _deltanet_branch(x, w_all, conv_w, alog_row, dtb_row, normw_row):
    b, s, d = x.shape
    tile = DN_TILE
    sub = DN_SUB
    next_ring, long_ring = 2 * sub, 4 * sub
    const = lambda bi, ti: (0, 0)
    return pl.pallas_call(
        _dn_kernel,
        grid=(b, s // tile),
        in_specs=[
            pl.BlockSpec((None, tile, d), lambda bi, ti: (bi, ti, 0)),
            _w_all_spec("dn_qkv", d),
            _w_all_spec("dn_z", d),
            _w_all_spec("dn_ba", d),
            pl.BlockSpec(conv_w.shape, const),
            pl.BlockSpec(alog_row.shape, const),
            pl.BlockSpec(dtb_row.shape, const),
            pl.BlockSpec(normw_row.shape, const),
        ],
        out_specs=pl.BlockSpec((None, tile, DN_WIDTH), lambda bi, ti: (bi, ti, 0)),
        out_shape=jax.ShapeDtypeStruct((b, s, DN_WIDTH), BF16),
        scratch_shapes=[
            pltpu.VMEM((sub, d), BF16),
            pltpu.VMEM((SUBLANES + sub, 3 * DN_WIDTH), F32),
            pltpu.VMEM((next_ring, DN_WIDTH), F32),
            pltpu.VMEM((next_ring, DN_WIDTH), F32),
            pltpu.VMEM((next_ring, DN_WIDTH), F32),
            pltpu.VMEM((long_ring, DN_WIDTH), F32),
            pltpu.VMEM((long_ring, LANES), F32),
            pltpu.VMEM((long_ring, LANES), F32),
            pltpu.VMEM((LANES, long_ring), F32),
            pltpu.VMEM((next_ring, DN_WIDTH), F32),
            pltpu.VMEM((next_ring, DN_WIDTH), BF16),
            pltpu.VMEM((next_ring, DN_WIDTH), BF16),
            pltpu.VMEM((next_ring, DN_WIDTH), BF16),
            pltpu.VMEM((next_ring, DN_WIDTH), BF16),
            pltpu.VMEM((DN_HEADS, DN_HEAD_DIM, DN_HEAD_DIM), F32),
        ],
        compiler_params=pltpu.CompilerParams(
            dimension_semantics=("arbitrary", "arbitrary"), vmem_limit_bytes=VMEM_LIMIT_BYTES),
        name="deltanet_branch",
    )(x, w_all, w_all, w_all, conv_w, alog_row, dtb_row, normw_row)


def _swa_kernel(sinks_ref, x_ref, wq_ref, wkv_ref, wz_ref, rope_off_ref, rope_start_ref, y_ref,
                kband, vband, q_s, z_s):
    t = pl.program_id(1)
    w = WINDOW
    tile = x_ref.shape[0]

    @pl.when(t == 0)
    def _():
        kband[:, 0:w, :] = jnp.zeros((2 * SWA_KV_HEADS, w, LANES), BF16)
        vband[:, 0:w, :] = jnp.zeros((2 * SWA_KV_HEADS, w, LANES), BF16)

    half = ROPE_DIM // 2
    sub = SWA_SUB
    n_sub = tile // sub
    n_pairs = SWA_Q_HEADS // 2
    group = SWA_Q_HEADS // SWA_KV_HEADS
    lane = lax.broadcasted_iota(jnp.int32, (sub, LANES), 1)
    lo = lane < SWA_HEAD_DIM
    head_lane = lane % SWA_HEAD_DIM

    def ring_rows(j, offset=0, size=sub):
        start = (j % (q_s.shape[0] // sub)) * sub + offset
        return slice(start, start + size)

    qi = lax.broadcasted_iota(jnp.int32, (w, 2 * w), 0)
    kj = lax.broadcasted_iota(jnp.int32, (w, 2 * w), 1)
    in_band = (kj > qi) & (kj <= qi + w)
    bias = jnp.where(in_band, 0.0, MASK_VALUE).astype(F32)
    bias_first = jnp.where(in_band & (kj >= jnp.where(t == 0, w, 0)), 0.0, MASK_VALUE).astype(F32)

    def front(j):
        r0 = j * sub
        rows = slice(r0, r0 + sub)
        cos_o = rope_off_ref[rows, 0:LANES]
        sin_o = rope_off_ref[rows, LANES:2 * LANES]
        cos_s = rope_start_ref[:, 0:LANES]
        sin_s = rope_start_ref[:, LANES:2 * LANES]
        cos_p = cos_s * cos_o - sin_s * sin_o
        sin_p = sin_s * cos_o + cos_s * sin_o
        sin_a = jnp.where(head_lane < half, -sin_p, 0.0)
        sin_b = jnp.where(head_lane >= half, sin_p, 0.0)

        def rope(v):
            return v * cos_p + pltpu.roll(v, LANES - half, 1) * sin_a + pltpu.roll(v, half, 1) * sin_b

        xb = x_ref[rows, :].astype(BF16)
        q = jnp.dot(xb, wq_ref[...], preferred_element_type=F32)
        yield
        for pair in range(n_pairs):
            p0 = pair * LANES
            q_s[ring_rows(j), p0:p0 + LANES] = rope(q[:, p0:p0 + LANES]).astype(BF16)
        yield
        kv = jnp.dot(xb, wkv_ref[...], preferred_element_type=F32)
        z_s[ring_rows(j), :] = jnp.dot(xb, wz_ref[...], preferred_element_type=F32)
        yield
        k = rope(kv[:, :LANES])
        v = kv[:, LANES:]
        band_rows = slice(w + r0, w + r0 + sub)
        for src, band in ((k, kband), (v, vband)):
            swapped = pltpu.roll(src, SWA_HEAD_DIM, 1)
            band[0, band_rows, :] = jnp.where(lo, src, 0.0).astype(BF16)
            band[1, band_rows, :] = jnp.where(lo, 0.0, swapped).astype(BF16)
            band[2, band_rows, :] = jnp.where(lo, swapped, 0.0).astype(BF16)
            band[3, band_rows, :] = jnp.where(lo, 0.0, src).astype(BF16)
        yield

    front_steps = 4

    slabs_per_group = group // 2
    upper_rows = lax.broadcasted_iota(jnp.int32, (slabs_per_group * w, 1), 0) >= w

    def kv_group(r0, staged, g, blk_bias):
        slabs = [slice((slabs_per_group * g + i) * LANES, (slabs_per_group * g + i + 1) * LANES)
                 for i in range(slabs_per_group)]
        q_rows = jnp.concatenate([q_s[staged, cols] for cols in slabs], axis=0)
        bias_rows = jnp.concatenate([blk_bias] * slabs_per_group, axis=0)
        scores, sinks = [], []
        for hf in range(2):
            heads_hf = [group * g + 2 * i + hf for i in range(slabs_per_group)]
            sinks.append(jnp.where(upper_rows, sinks_ref[heads_hf[1]], sinks_ref[heads_hf[0]]))
            scores.append(_mm_nt(q_rows, kband[2 * g + hf, r0:r0 + 2 * w, :]) + bias_rows)
        yield
        probs, denoms = [], []
        for s, sink in zip(scores, sinks):
            m = jnp.maximum(jnp.max(jnp.maximum(s[:, :w], s[:, w:]), axis=-1, keepdims=True), sink)
            p = jnp.exp(s - m)
            denoms.append(jnp.sum(p[:, :w] + p[:, w:], axis=-1, keepdims=True) + jnp.exp(sink - m))
            probs.append(p.astype(BF16))
        yield
        acc = None
        for hf in range(2):
            o = _mm(probs[hf], vband[2 * g + hf, r0:r0 + 2 * w, :]) / denoms[hf]
            acc = o if acc is None else acc + o
        for i, cols in enumerate(slabs):
            y_ref[r0:r0 + w, cols] = (
                acc[i * w:(i + 1) * w] * _silu(z_s[staged, cols])).astype(y_ref.dtype)
        yield

    def attend(j):
        items = [(j * sub + b * w, ring_rows(j, b * w, w), g)
                 for b in range(sub // w) for g in range(SWA_KV_HEADS)]
        for g0 in range(0, len(items), SWA_GROUPS_IN_FLIGHT):
            streams = [kv_group(r0, staged, g, bias_first if r0 == 0 else bias)
                       for r0, staged, g in items[g0:g0 + SWA_GROUPS_IN_FLIGHT]]
            for _ in range(3):
                for stream in streams:
                    next(stream)
                yield

    attend_steps = 3 * (sub // w) * SWA_KV_HEADS // SWA_GROUPS_IN_FLIGHT

    stages = ((front, front_steps), (attend, attend_steps))
    for slot in range(n_sub + len(stages) - 1):
        _interleave([(stage(slot - k), steps) for k, (stage, steps) in enumerate(stages)
                     if 0 <= slot - k < n_sub])

    kband[:, 0:w, :] = kband[:, tile:tile + w, :]
    vband[:, 0:w, :] = vband[:, tile:tile + w, :]


def _swa_branch(x, sinks, w_all):
    b, s, d = x.shape
    w = WINDOW
    tile = SWA_TILE
    rope_offsets, rope_starts = _rope_tables(s, tile)
    return pl.pallas_call(
        _swa_kernel,
        grid=(b, s // tile),
        in_specs=[
            pl.BlockSpec(memory_space=pltpu.SMEM),
            pl.BlockSpec((None, tile, d), lambda bi, ti: (bi, ti, 0)),
            _w_all_spec("swa_q", d),
            _w_all_spec("swa_kv", d),
            _w_all_spec("swa_z", d),
            pl.BlockSpec((tile, 2 * LANES), lambda bi, ti: (0, 0)),
            pl.BlockSpec((None, 1, 2 * LANES), lambda bi, ti: (ti, 0, 0)),
        ],
        out_specs=pl.BlockSpec((None, tile, SWA_WIDTH), lambda bi, ti: (bi, ti, 0)),
        out_shape=jax.ShapeDtypeStruct((b, s, SWA_WIDTH), BF16),
        scratch_shapes=[
            pltpu.VMEM((2 * SWA_KV_HEADS, w + tile, LANES), BF16),
            pltpu.VMEM((2 * SWA_KV_HEADS, w + tile, LANES), BF16),
            pltpu.VMEM((2 * SWA_SUB, SWA_WIDTH), BF16),
            pltpu.VMEM((2 * SWA_SUB, SWA_WIDTH), F32),
        ],
        compiler_params=pltpu.CompilerParams(
            dimension_semantics=("arbitrary", "arbitrary"), vmem_limit_bytes=VMEM_LIMIT_BYTES),
        name="swa_branch",
    )(sinks, x, w_all, w_all, w_all, rope_offsets, rope_starts)


def _rope_tables(seq, tile):
    inv_freq = ROPE_THETA ** (-jnp.arange(0, ROPE_DIM, 2, dtype=F32) / ROPE_DIM)

    def lane_pattern(positions):
        ang = positions[:, None] * inv_freq[None, :]
        n = positions.shape[0]
        rest = SWA_HEAD_DIM - ROPE_DIM
        reps = LANES // SWA_HEAD_DIM
        cos = jnp.concatenate([jnp.cos(ang)] * 2 + [jnp.ones((n, rest), F32)], axis=1)
        sin = jnp.concatenate([jnp.sin(ang)] * 2 + [jnp.zeros((n, rest), F32)], axis=1)
        return jnp.concatenate([jnp.tile(cos, (1, reps)), jnp.tile(sin, (1, reps))], axis=1)

    offsets = lane_pattern(jnp.arange(tile, dtype=F32))
    starts = lane_pattern(jnp.arange(seq // tile, dtype=F32) * tile)
    return offsets, starts.reshape(seq // tile, 1, 2 * LANES)


def _merge_kernel(alpha, x_ref, ya_ref, yb_ref, wg_ref, wa_ref, wb_ref, wo_ref, lng_ref, lnb_ref, o_ref,
                  merged_s):
    tile = x_ref.shape[0]
    sub = MERGE_SUB
    n_sub = tile // sub

    def gated_merge(j):
        rows = slice(j * sub, (j + 1) * sub)
        xb = x_ref[rows, :].astype(BF16)
        gate_a = jax.nn.sigmoid(jnp.dot(xb, wg_ref[:, :D_MODEL], preferred_element_type=F32))
        pa = jnp.dot(ya_ref[rows, :], wa_ref[...], preferred_element_type=F32)
        yield
        gate_b = jax.nn.sigmoid(jnp.dot(xb, wg_ref[:, D_MODEL:], preferred_element_type=F32))
        pb = jnp.dot(yb_ref[rows, :], wb_ref[...], preferred_element_type=F32)
        yield
        merged_s[rows, :] = (gate_a * pa + gate_b * pb).astype(BF16)
        yield

    def project_norm(j):
        rows = slice(j * sub, (j + 1) * sub)
        out = jnp.dot(merged_s[rows, :], wo_ref[...], preferred_element_type=F32)
        yield
        r = alpha * x_ref[rows, :] + out
        mu = jnp.mean(r, axis=-1, keepdims=True)
        cen = r - mu
        var = jnp.mean(cen * cen, axis=-1, keepdims=True)
        o_ref[rows, :] = (cen * lax.rsqrt(var + LN_EPS) * lng_ref[...] + lnb_ref[...]).astype(o_ref.dtype)
        yield

    stages = ((gated_merge, 3), (project_norm, 2))
    for slot in range(n_sub + len(stages) - 1):
        _interleave([(stage(slot - k), steps) for k, (stage, steps) in enumerate(stages)
                     if 0 <= slot - k < n_sub])


def _merge(x2, ya2, yb2, w_all, wa, wb, wo, lng_row, lnb_row, alpha):
    n, d = x2.shape
    tile = MERGE_TILE
    const = lambda i: (0, 0)
    row = lambda i: (i, 0)
    return pl.pallas_call(
        functools.partial(_merge_kernel, alpha),
        grid=(n // tile,),
        in_specs=[
            pl.BlockSpec((tile, d), row),
            pl.BlockSpec((tile, ya2.shape[1]), row),
            pl.BlockSpec((tile, yb2.shape[1]), row),
            _w_all_spec("gates", d),
            pl.BlockSpec(wa.shape, const),
            pl.BlockSpec(wb.shape, const),
            pl.BlockSpec(wo.shape, const),
            pl.BlockSpec(lng_row.shape, const),
            pl.BlockSpec(lnb_row.shape, const),
        ],
        out_specs=pl.BlockSpec((tile, d), row),
        out_shape=jax.ShapeDtypeStruct((n, d), x2.dtype),
        scratch_shapes=[pltpu.VMEM((tile, d), BF16)],
        compiler_params=pltpu.CompilerParams(
            dimension_semantics=("arbitrary",), vmem_limit_bytes=VMEM_LIMIT_BYTES),
        name="merge_out_norm",
    )(x2, ya2, yb2, w_all, wa, wb, wo, lng_row, lnb_row)


def _lane_row(vec, offset):
    n = vec.shape[0]
    return jnp.pad(vec.astype(F32), (offset, LANES - offset - n)).reshape(1, LANES)


def _pack_w_in(w_in):
    o = _OFF_SWA
    groups = {
        "dn_qkv": w_in[:, _OFF_QKV:_OFF_DN_Z],
        "dn_z": w_in[:, _OFF_DN_Z:_OFF_DN_BA],
        "gates": w_in[:, _OFF_GATE:],
        "swa_q": w_in[:, o:o + SWA_WIDTH] * (SWA_HEAD_DIM ** -0.5),
        "swa_z": w_in[:, o + SWA_WIDTH + 2 * SWA_KV_WIDTH:_OFF_GATE],
        "swa_kv": w_in[:, o + SWA_WIDTH:o + SWA_WIDTH + 2 * SWA_KV_WIDTH],
        "dn_ba": jnp.pad(w_in[:, _OFF_DN_BA:_OFF_SWA], ((0, 0), (0, LANES - 2 * DN_HEADS))),
    }
    assert list(groups) == list(_W_ALL_LAYOUT)
    return jnp.concatenate([g.astype(BF16) for g in groups.values()], axis=1)


def _layer(x, w_in, conv_w, a_log, dt_bias, dn_norm_w, sinks, w_branch, w_out, ln_g, ln_b, alpha):
    b, s, d = x.shape
    w_all = _pack_w_in(w_in)
    y_a = _deltanet_branch(x, w_all, conv_w.astype(F32), _lane_row(a_log, DN_HEADS),
                           _lane_row(dt_bias, DN_HEADS), dn_norm_w.astype(F32).reshape(1, DN_HEAD_DIM))
    y_b = _swa_branch(x, sinks.astype(F32), w_all)
    w_ab = w_branch.astype(BF16)
    out = _merge(x.reshape(b * s, d), y_a.reshape(b * s, DN_WIDTH), y_b.reshape(b * s, SWA_WIDTH),
                 w_all, w_ab[0], w_ab[1], w_out.astype(BF16),
                 ln_g.astype(F32).reshape(1, d), ln_b.astype(F32).reshape(1, d), alpha)
    return out.reshape(b, s, d)


def kernel(x, w_in, conv_w, a_log, dt_bias, dn_norm_w, sinks, w_branch, w_out, ln_g, ln_b):
    depth = w_in.shape[0]
    alpha = (2.0 * depth) ** 0.25
    for layer in range(depth):
        x = _layer(x, w_in[layer], conv_w[layer], a_log[layer], dt_bias[layer], dn_norm_w[layer],
                   sinks[layer], w_branch[layer], w_out[layer], ln_g[layer], ln_b[layer], alpha)
    return x
```

```python
import functools

import jax
import jax.numpy as jnp
from jax import lax
from jax.experimental import pallas as pl
from jax.experimental.pallas import tpu as pltpu

F32 = jnp.float32
BF16 = jnp.bfloat16

D_MODEL = 1024
DN_HEADS = 4
DN_HEAD_DIM = 128
DN_WIDTH = DN_HEADS * DN_HEAD_DIM
CONV_WIDTH = 4
CHUNK = 64
SWA_Q_HEADS = 8
SWA_KV_HEADS = 2
SWA_HEAD_DIM = 64
SWA_WIDTH = SWA_Q_HEADS * SWA_HEAD_DIM
SWA_KV_WIDTH = SWA_KV_HEADS * SWA_HEAD_DIM
WINDOW = 128
ROPE_THETA = 500000.0
ROPE_DIM = SWA_HEAD_DIM // 4
LN_EPS = 1e-5
NORM_EPS = 1e-6
MASK_VALUE = -1e30

LANES = 128
SUBLANES = 8
VMEM_LIMIT_BYTES = 48 * 1024 * 1024

DN_TILE = 1024
DN_SUB = 256
DN_PROJ_COLS = 256
SWA_TILE = 1024
SWA_SUB = 256
SWA_GROUPS_IN_FLIGHT = 2
MERGE_TILE = 1024
MERGE_SUB = 256
assert CONV_WIDTH == 4

_OFF_QKV = 0
_OFF_DN_Z = 3 * DN_WIDTH
_OFF_DN_BA = _OFF_DN_Z + DN_WIDTH
_OFF_SWA = _OFF_DN_BA + 2 * DN_HEADS
_SWA_COLS = SWA_WIDTH + 2 * SWA_KV_WIDTH + SWA_WIDTH
_OFF_GATE = _OFF_SWA + _SWA_COLS


def _make_w_all_layout():
    widths = (("dn_qkv", 3 * DN_WIDTH), ("dn_z", DN_WIDTH), ("gates", 2 * D_MODEL), ("swa_q", SWA_WIDTH),
              ("swa_z", SWA_WIDTH), ("swa_kv", 2 * SWA_KV_WIDTH), ("dn_ba", LANES))
    layout, off = {}, 0
    for name, width in widths:
        assert off % width == 0
        layout[name] = (off, width)
        off += width
    return layout


_W_ALL_LAYOUT = _make_w_all_layout()


def _mm(a, b):
    return jnp.dot(a.astype(BF16), b.astype(BF16), preferred_element_type=F32)


def _mm_nt(a, b):
    return lax.dot_general(a.astype(BF16), b.astype(BF16), (((1,), (1,)), ((), ())),
                           preferred_element_type=F32)


def _mm_tn(a, b):
    return lax.dot_general(a.astype(BF16), b.astype(BF16), (((0,), (0,)), ((), ())),
                           preferred_element_type=F32)


def _silu(v):
    return v * jax.nn.sigmoid(v)


def _interleave(streams):
    live = [[gen, 0, steps] for gen, steps in streams]
    while live:
        entry = min(live, key=lambda e: e[1] / e[2])
        try:
            next(entry[0])
            entry[1] += 1
        except StopIteration:
            live.remove(entry)


def _dn_kernel(x_ref, wqkv_ref, wz_ref, wba_ref, convw_ref, alog_ref, dtb_ref, normw_ref, y_ref,
               xb_s, hbuf, q_s, k_s, v_s, z_s, beta_s, gc_s, gct_s, u_s, w_s, qd_s, kt_s, a_s, state):
    t = pl.program_id(1)
    tile = x_ref.shape[0]
    sub = DN_SUB
    n_sub = tile // sub
    halo = SUBLANES
    pair = 2 * CHUNK
    pairs_per_sub = sub // pair
    heads = range(DN_HEADS)

    @pl.when(t == 0)
    def _():
        hbuf[0:halo, :] = jnp.zeros((halo, 3 * DN_WIDTH), F32)
        state[...] = jnp.zeros_like(state)

    row = lax.broadcasted_iota(jnp.int32, (CHUNK, pair), 0)
    lane = lax.broadcasted_iota(jnp.int32, (CHUNK, pair), 1)
    col = lane % CHUNK
    left = lane < CHUNK
    causal = row >= col
    strict = row > col
    xor_ij = row ^ col
    eye = jnp.where(row == col, 1.0, 0.0).astype(F32)
    first_chunk = lax.broadcasted_iota(jnp.int32, (pair, LANES), 0) < CHUNK
    pos = lax.broadcasted_iota(jnp.int32, (sub, LANES), 0) % CHUNK

    def head_cols(h):
        return slice(h * DN_HEAD_DIM, (h + 1) * DN_HEAD_DIM)

    def side_by_side(m):
        return jnp.where(left, m[:CHUNK], m[CHUNK:])

    def block_diag(m):
        zero = jnp.zeros_like(m)
        return jnp.concatenate([jnp.where(left, m, zero), jnp.where(left, zero, m)], axis=0)

    def front(j):
        r0 = j * sub
        rows = slice(r0, r0 + sub)
        xb_s[rows, :] = x_ref[rows, :].astype(BF16)
        ba = jnp.dot(xb_s[rows, :], wba_ref[...], preferred_element_type=F32)
        yield
        beta_s[rows, :] = jax.nn.sigmoid(ba)
        xg = ba + dtb_ref[...]
        softplus = jnp.maximum(xg, 0.0) + jnp.log1p(jnp.exp(-jnp.abs(xg)))
        gc = -jnp.exp(alog_ref[...]) * softplus
        step = 1
        while step < CHUNK:
            gc = gc + jnp.where(pos >= step, pltpu.roll(gc, step, 0), 0.0)
            step *= 2
        gc_s[rows, :] = gc
        gct_s[:, rows] = gc.T
        yield
        dests = (q_s, k_s, v_s)
        slabs_per_chunk = DN_PROJ_COLS // DN_HEAD_DIM
        for s in range(3 * DN_HEADS):
            if s % slabs_per_chunk == 0:
                chunk = slice(s * DN_HEAD_DIM, s * DN_HEAD_DIM + DN_PROJ_COLS)
                hbuf[halo + r0:halo + r0 + sub, chunk] = jnp.dot(
                    xb_s[rows, :], wqkv_ref[:, chunk], preferred_element_type=F32)
                z0 = (s // slabs_per_chunk) * DN_PROJ_COLS
                if z0 < DN_WIDTH:
                    z_s[rows, z0:z0 + DN_PROJ_COLS] = jnp.dot(
                        xb_s[rows, :], wz_ref[:, z0:z0 + DN_PROJ_COLS], preferred_element_type=F32)
                yield
            cols = slice(s * DN_HEAD_DIM, (s + 1) * DN_HEAD_DIM)
            ext = hbuf[r0:r0 + halo + sub, cols]
            prev = pltpu.roll(ext, 1, 0)
            older = convw_ref[0:1, cols] * prev + convw_ref[1:2, cols] * ext
            newer = convw_ref[2:3, cols] * prev + convw_ref[3:4, cols] * ext
            acc = (pltpu.roll(older, 2, 0) + newer)[halo:, :]
            yield
            a = _silu(acc)
            if s < 2 * DN_HEADS:
                a = a * lax.rsqrt(jnp.sum(a * a, axis=-1, keepdims=True) + NORM_EPS)
                if s < DN_HEADS:
                    a = a * (DN_HEAD_DIM ** -0.5)
            dests[s // DN_HEADS][rows, head_cols(s % DN_HEADS)] = a
            yield
        if j == n_sub - 1:
            hbuf[0:halo, :] = hbuf[tile:tile + halo, :]

    front_steps = 2 + 3 * DN_WIDTH // DN_PROJ_COLS + 2 * 3 * DN_HEADS

    def factor(j):
        problems = [(j * sub + p * pair, h) for p in range(pairs_per_sub) for h in heads]
        lows, rhs = [], []
        for r0, h in problems:
            rows = slice(r0, r0 + pair)
            la = DN_HEADS + h
            if h == 0:
                gc_p = gc_s[rows, :]
                e_gc = jnp.exp(gc_p)
                g_end = jnp.where(first_chunk, gc_p[CHUNK - 1:CHUNK, :], gc_p[pair - 1:pair, :])
                e_tail = jnp.exp(g_end - gc_p)
                beta_p = beta_s[rows, :]
            g_col = gc_p[:, la:la + 1]
            g_row = gct_s[la:la + 1, rows]
            b_col = beta_p[:, h:h + 1]
            eg_col = e_gc[:, la:la + 1]
            qh = q_s[rows, head_cols(h)]
            kh = k_s[rows, head_cols(h)]
            vh = v_s[rows, head_cols(h)]
            g_diff = jnp.where(left, g_col[:CHUNK], g_col[CHUNK:]) - g_row
            decay = jnp.where(causal, jnp.exp(jnp.where(causal, g_diff, 0.0)), 0.0)
            kb = kh * b_col
            kq = _mm_nt(jnp.concatenate([kb, qh], axis=0), kh)
            lows.append(jnp.where(strict, side_by_side(kq[:pair]) * decay, 0.0))
            a_s[rows, head_cols(h)] = block_diag((side_by_side(kq[pair:]) * decay).astype(BF16))
            rhs.append(jnp.concatenate([vh * b_col, kb * eg_col], axis=1).astype(BF16))
            qd_s[rows, head_cols(h)] = (qh * eg_col).astype(BF16)
            kt_s[rows, head_cols(h)] = (kh * e_tail[:, la:la + 1]).astype(BF16)
            yield
        invs = [eye - jnp.where(xor_ij == 1, low, 0.0) for low in lows]
        level = 1
        while (1 << level) < CHUNK:
            joins = (xor_ij >> level) == 1
            cs = [block_diag(jnp.where(joins, low, 0.0).astype(BF16)) for low in lows]
            xs = [inv.astype(BF16) for inv in invs]
            xcs = []
            for x, c in zip(xs, cs):
                xcs.append(_mm(x, c))
                yield
            for i, (xc, x) in enumerate(zip(xcs, xs)):
                invs[i] = invs[i] - _mm(xc, block_diag(x))
                yield
            level += 1
        for (r0, h), t_inv, r in zip(problems, invs, rhs):
            rows = slice(r0, r0 + pair)
            uw = _mm(block_diag(t_inv.astype(BF16)), r)
            u_s[rows, head_cols(h)] = uw[:, :DN_HEAD_DIM]
            w_s[rows, head_cols(h)] = uw[:, DN_HEAD_DIM:].astype(BF16)
            yield

    levels = CHUNK.bit_length() - 2
    factor_steps = (2 + 2 * levels) * pairs_per_sub * DN_HEADS

    def recur(j):
        for p in range(pairs_per_sub):
            p0 = j * sub + p * pair
            v_new = [[None, None] for _ in heads]
            q_state = [[None, None] for _ in heads]
            for cc in range(2):
                r0 = p0 + cc * CHUNK
                rows = slice(r0, r0 + CHUNK)
                e_end = jnp.exp(gc_s[r0 + CHUNK - 1:r0 + CHUNK, :])
                s_in = [state[h] for h in heads]
                prods = []
                for h in heads:
                    lhs = jnp.concatenate([w_s[rows, head_cols(h)], qd_s[rows, head_cols(h)]], axis=0)
                    prods.append(_mm(lhs, s_in[h]))
                for h in heads:
                    v_new[h][cc] = u_s[rows, head_cols(h)] - prods[h][:CHUNK]
                    q_state[h][cc] = prods[h][CHUNK:]
                yield
                for h in heads:
                    la = DN_HEADS + h
                    state[h] = (s_in[h] * e_end[:, la:la + 1]
                                + _mm_tn(kt_s[rows, head_cols(h)], v_new[h][cc]))
                yield
            rows = slice(p0, p0 + pair)
            for h in heads:
                o = (jnp.concatenate(q_state[h], axis=0)
                     + _mm(a_s[rows, head_cols(h)], jnp.concatenate(v_new[h], axis=0)))
                o = o * lax.rsqrt(jnp.mean(o * o, axis=-1, keepdims=True) + NORM_EPS) * normw_ref[...]
                y_ref[rows, head_cols(h)] = (o * _silu(z_s[rows, head_cols(h)])).astype(y_ref.dtype)
                yield

    recur_steps = (4 + DN_HEADS) * pairs_per_sub

    stages = ((front, front_steps), (factor, factor_steps), (recur, recur_steps))
    for slot in range(n_sub + len(stages) - 1):
        _interleave([(stage(slot - k), steps) for k, (stage, steps) in enumerate(stages)
                     if 0 <= slot - k < n_sub])


def _w_all_spec(name, rows):
    off, width = _W_ALL_LAYOUT[name]
    return pl.BlockSpec((rows, width), lambda *_: (0, off // width))


def _deltanet_branch(x, w_all, conv_w, alog_row, dtb_row, normw_row):
    b, s, d = x.shape
    tile = DN_TILE
    assert d == D_MODEL and s % tile == 0 and tile % DN_SUB == 0 and DN_SUB % (2 * CHUNK) == 0
    assert (3 * DN_WIDTH) % DN_PROJ_COLS == 0 and DN_PROJ_COLS % DN_HEAD_DIM == 0
    const = lambda bi, ti: (0, 0)
    return pl.pallas_call(
        _dn_kernel,
        grid=(b, s // tile),
        in_specs=[
            pl.BlockSpec((None, tile, d), lambda bi, ti: (bi, ti, 0)),
            _w_all_spec("dn_qkv", d),
            _w_all_spec("dn_z", d),
            _w_all_spec("dn_ba", d),
            pl.BlockSpec(conv_w.shape, const),
            pl.BlockSpec(alog_row.shape, const),
            pl.BlockSpec(dtb_row.shape, const),
            pl.BlockSpec(normw_row.shape, const),
        ],
        out_specs=pl.BlockSpec((None, tile, DN_WIDTH), lambda bi, ti: (bi, ti, 0)),
        out_shape=jax.ShapeDtypeStruct((b, s, DN_WIDTH), BF16),
        scratch_shapes=[
            pltpu.VMEM((tile, d), BF16),
            pltpu.VMEM((tile + SUBLANES, 3 * DN_WIDTH), F32),
            pltpu.VMEM((tile, DN_WIDTH), F32),
            pltpu.VMEM((tile, DN_WIDTH), F32),
            pltpu.VMEM((tile, DN_WIDTH), F32),
            pltpu.VMEM((tile, DN_WIDTH), F32),
            pltpu.VMEM((tile, LANES), F32),
            pltpu.VMEM((tile, LANES), F32),
            pltpu.VMEM((LANES, tile), F32),
            pltpu.VMEM((tile, DN_WIDTH), F32),
            pltpu.VMEM((tile, DN_WIDTH), BF16),
            pltpu.VMEM((tile, DN_WIDTH), BF16),
            pltpu.VMEM((tile, DN_WIDTH), BF16),
            pltpu.VMEM((tile, DN_WIDTH), BF16),
            pltpu.VMEM((DN_HEADS, DN_HEAD_DIM, DN_HEAD_DIM), F32),
        ],
        compiler_params=pltpu.CompilerParams(
            dimension_semantics=("arbitrary", "arbitrary"), vmem_limit_bytes=VMEM_LIMIT_BYTES),
        name="deltanet_branch",
    )(x, w_all, w_all, w_all, conv_w, alog_row, dtb_row, normw_row)


def _swa_kernel(sinks_ref, x_ref, wq_ref, wkv_ref, wz_ref, rope_off_ref, rope_start_ref, y_ref,
                kband, vband, q_s, z_s):
    t = pl.program_id(1)
    w = WINDOW
    tile = x_ref.shape[0]

    @pl.when(t == 0)
    def _():
        kband[:, 0:w, :] = jnp.zeros((2 * SWA_KV_HEADS, w, LANES), BF16)
        vband[:, 0:w, :] = jnp.zeros((2 * SWA_KV_HEADS, w, LANES), BF16)

    half = ROPE_DIM // 2
    sub = SWA_SUB
    n_sub = tile // sub
    n_pairs = SWA_Q_HEADS // 2
    group = SWA_Q_HEADS // SWA_KV_HEADS
    lane = lax.broadcasted_iota(jnp.int32, (sub, LANES), 1)
    lo = lane < SWA_HEAD_DIM
    head_lane = lane % SWA_HEAD_DIM

    qi = lax.broadcasted_iota(jnp.int32, (w, 2 * w), 0)
    kj = lax.broadcasted_iota(jnp.int32, (w, 2 * w), 1)
    in_band = (kj > qi) & (kj <= qi + w)
    bias = jnp.where(in_band, 0.0, MASK_VALUE).astype(F32)
    bias_first = jnp.where(in_band & (kj >= jnp.where(t == 0, w, 0)), 0.0, MASK_VALUE).astype(F32)

    def front(j):
        r0 = j * sub
        rows = slice(r0, r0 + sub)
        cos_o = rope_off_ref[rows, 0:LANES]
        sin_o = rope_off_ref[rows, LANES:2 * LANES]
        cos_s = rope_start_ref[:, 0:LANES]
        sin_s = rope_start_ref[:, LANES:2 * LANES]
        cos_p = cos_s * cos_o - sin_s * sin_o
        sin_p = sin_s * cos_o + cos_s * sin_o
        sin_a = jnp.where(head_lane < half, -sin_p, 0.0)
        sin_b = jnp.where(head_lane >= half, sin_p, 0.0)

        def rope(v):
            return v * cos_p + pltpu.roll(v, LANES - half, 1) * sin_a + pltpu.roll(v, half, 1) * sin_b

        xb = x_ref[rows, :].astype(BF16)
        q = jnp.dot(xb, wq_ref[...], preferred_element_type=F32)
        yield
        for pair in range(n_pairs):
            p0 = pair * LANES
            q_s[rows, p0:p0 + LANES] = rope(q[:, p0:p0 + LANES]).astype(BF16)
        yield
        kv = jnp.dot(xb, wkv_ref[...], preferred_element_type=F32)
        z_s[rows, :] = jnp.dot(xb, wz_ref[...], preferred_element_type=F32)
        yield
        k = rope(kv[:, :LANES])
        v = kv[:, LANES:]
        band_rows = slice(w + r0, w + r0 + sub)
        for src, band in ((k, kband), (v, vband)):
            swapped = pltpu.roll(src, SWA_HEAD_DIM, 1)
            band[0, band_rows, :] = jnp.where(lo, src, 0.0).astype(BF16)
            band[1, band_rows, :] = jnp.where(lo, 0.0, swapped).astype(BF16)
            band[2, band_rows, :] = jnp.where(lo, swapped, 0.0).astype(BF16)
            band[3, band_rows, :] = jnp.where(lo, 0.0, src).astype(BF16)
        yield

    front_steps = 4

    slabs_per_group = group // 2
    upper_rows = lax.broadcasted_iota(jnp.int32, (slabs_per_group * w, 1), 0) >= w

    def kv_group(r0, g, blk_bias):
        slabs = [slice((slabs_per_group * g + i) * LANES, (slabs_per_group * g + i + 1) * LANES)
                 for i in range(slabs_per_group)]
        q_rows = jnp.concatenate([q_s[r0:r0 + w, cols] for cols in slabs], axis=0)
        bias_rows = jnp.concatenate([blk_bias] * slabs_per_group, axis=0)
        scores, sinks = [], []
        for hf in range(2):
            heads_hf = [group * g + 2 * i + hf for i in range(slabs_per_group)]
            sinks.append(jnp.where(upper_rows, sinks_ref[heads_hf[1]], sinks_ref[heads_hf[0]]))
            scores.append(_mm_nt(q_rows, kband[2 * g + hf, r0:r0 + 2 * w, :]) + bias_rows)
        yield
        probs, denoms = [], []
        for s, sink in zip(scores, sinks):
            m = jnp.maximum(jnp.max(jnp.maximum(s[:, :w], s[:, w:]), axis=-1, keepdims=True), sink)
            p = jnp.exp(s - m)
            denoms.append(jnp.sum(p[:, :w] + p[:, w:], axis=-1, keepdims=True) + jnp.exp(sink - m))
            probs.append(p.astype(BF16))
        yield
        acc = None
        for hf in range(2):
            o = _mm(probs[hf], vband[2 * g + hf, r0:r0 + 2 * w, :]) / denoms[hf]
            acc = o if acc is None else acc + o
        for i, cols in enumerate(slabs):
            y_ref[r0:r0 + w, cols] = (
                acc[i * w:(i + 1) * w] * _silu(z_s[r0:r0 + w, cols])).astype(y_ref.dtype)
        yield

    def attend(j):
        items = [(j * sub + b * w, g) for b in range(sub // w) for g in range(SWA_KV_HEADS)]
        for g0 in range(0, len(items), SWA_GROUPS_IN_FLIGHT):
            streams = [kv_group(r0, g, bias_first if r0 == 0 else bias)
                       for r0, g in items[g0:g0 + SWA_GROUPS_IN_FLIGHT]]
            for _ in range(3):
                for stream in streams:
                    next(stream)
                yield

    attend_steps = 3 * (sub // w) * SWA_KV_HEADS // SWA_GROUPS_IN_FLIGHT

    stages = ((front, front_steps), (attend, attend_steps))
    for slot in range(n_sub + len(stages) - 1):
        _interleave([(stage(slot - k), steps) for k, (stage, steps) in enumerate(stages)
                     if 0 <= slot - k < n_sub])

    kband[:, 0:w, :] = kband[:, tile:tile + w, :]
    vband[:, 0:w, :] = vband[:, tile:tile + w, :]


def _swa_branch(x, sinks, w_all):
    b, s, d = x.shape
    w = WINDOW
    tile = SWA_TILE
    assert d == D_MODEL and s % tile == 0 and tile % SWA_SUB == 0 and SWA_SUB % w == 0
    assert SWA_Q_HEADS // SWA_KV_HEADS == 4 and 2 * SWA_HEAD_DIM == LANES
    assert (SWA_SUB // w * SWA_KV_HEADS) % SWA_GROUPS_IN_FLIGHT == 0
    rope_offsets, rope_starts = _rope_tables(s, tile)
    return pl.pallas_call(
        _swa_kernel,
        grid=(b, s // tile),
        in_specs=[
            pl.BlockSpec(memory_space=pltpu.SMEM),
            pl.BlockSpec((None, tile, d), lambda bi, ti: (bi, ti, 0)),
            _w_all_spec("swa_q", d),
            _w_all_spec("swa_kv", d),
            _w_all_spec("swa_z", d),
            pl.BlockSpec((tile, 2 * LANES), lambda bi, ti: (0, 0)),
            pl.BlockSpec((None, 1, 2 * LANES), lambda bi, ti: (ti, 0, 0)),
        ],
        out_specs=pl.BlockSpec((None, tile, SWA_WIDTH), lambda bi, ti: (bi, ti, 0)),
        out_shape=jax.ShapeDtypeStruct((b, s, SWA_WIDTH), BF16),
        scratch_shapes=[
            pltpu.VMEM((2 * SWA_KV_HEADS, w + tile, LANES), BF16),
            pltpu.VMEM((2 * SWA_KV_HEADS, w + tile, LANES), BF16),
            pltpu.VMEM((tile, SWA_WIDTH), BF16),
            pltpu.VMEM((tile, SWA_WIDTH), F32),
        ],
        compiler_params=pltpu.CompilerParams(
            dimension_semantics=("arbitrary", "arbitrary"), vmem_limit_bytes=VMEM_LIMIT_BYTES),
        name="swa_branch",
    )(sinks, x, w_all, w_all, w_all, rope_offsets, rope_starts)


def _rope_tables(seq, tile):
    inv_freq = ROPE_THETA ** (-jnp.arange(0, ROPE_DIM, 2, dtype=F32) / ROPE_DIM)

    def lane_pattern(positions):
        ang = positions[:, None] * inv_freq[None, :]
        n = positions.shape[0]
        rest = SWA_HEAD_DIM - ROPE_DIM
        reps = LANES // SWA_HEAD_DIM
        cos = jnp.concatenate([jnp.cos(ang)] * 2 + [jnp.ones((n, rest), F32)], axis=1)
        sin = jnp.concatenate([jnp.sin(ang)] * 2 + [jnp.zeros((n, rest), F32)], axis=1)
        return jnp.concatenate([jnp.tile(cos, (1, reps)), jnp.tile(sin, (1, reps))], axis=1)

    offsets = lane_pattern(jnp.arange(tile, dtype=F32))
    starts = lane_pattern(jnp.arange(seq // tile, dtype=F32) * tile)
    return offsets, starts.reshape(seq // tile, 1, 2 * LANES)


def _merge_kernel(alpha, x_ref, ya_ref, yb_ref, wg_ref, wa_ref, wb_ref, wo_ref, lng_ref, lnb_ref, o_ref,
                  merged_s):
    tile = x_ref.shape[0]
    sub = MERGE_SUB
    n_sub = tile // sub

    def gated_merge(j):
        rows = slice(j * sub, (j + 1) * sub)
        xb = x_ref[rows, :].astype(BF16)
        gate_a = jax.nn.sigmoid(jnp.dot(xb, wg_ref[:, :D_MODEL], preferred_element_type=F32))
        pa = jnp.dot(ya_ref[rows, :], wa_ref[...], preferred_element_type=F32)
        yield
        gate_b = jax.nn.sigmoid(jnp.dot(xb, wg_ref[:, D_MODEL:], preferred_element_type=F32))
        pb = jnp.dot(yb_ref[rows, :], wb_ref[...], preferred_element_type=F32)
        yield
        merged_s[rows, :] = (gate_a * pa + gate_b * pb).astype(BF16)
        yield

    def project_norm(j):
        rows = slice(j * sub, (j + 1) * sub)
        out = jnp.dot(merged_s[rows, :], wo_ref[...], preferred_element_type=F32)
        yield
        r = alpha * x_ref[rows, :] + out
        mu = jnp.mean(r, axis=-1, keepdims=True)
        cen = r - mu
        var = jnp.mean(cen * cen, axis=-1, keepdims=True)
        o_ref[rows, :] = (cen * lax.rsqrt(var + LN_EPS) * lng_ref[...] + lnb_ref[...]).astype(o_ref.dtype)
        yield

    stages = ((gated_merge, 3), (project_norm, 2))
    for slot in range(n_sub + len(stages) - 1):
        _interleave([(stage(slot - k), steps) for k, (stage, steps) in enumerate(stages)
                     if 0 <= slot - k < n_sub])


def _merge(x2, ya2, yb2, w_all, wa, wb, wo, lng_row, lnb_row, alpha):
    n, d = x2.shape
    tile = MERGE_TILE
    assert d == D_MODEL and n % tile == 0 and tile % MERGE_SUB == 0
    const = lambda i: (0, 0)
    row = lambda i: (i, 0)
    return pl.pallas_call(
        functools.partial(_merge_kernel, alpha),
        grid=(n // tile,),
        in_specs=[
            pl.BlockSpec((tile, d), row),
            pl.BlockSpec((tile, ya2.shape[1]), row),
            pl.BlockSpec((tile, yb2.shape[1]), row),
            _w_all_spec("gates", d),
            pl.BlockSpec(wa.shape, const),
            pl.BlockSpec(wb.shape, const),
            pl.BlockSpec(wo.shape, const),
            pl.BlockSpec(lng_row.shape, const),
            pl.BlockSpec(lnb_row.shape, const),
        ],
        out_specs=pl.BlockSpec((tile, d), row),
        out_shape=jax.ShapeDtypeStruct((n, d), x2.dtype),
        scratch_shapes=[pltpu.VMEM((tile, d), BF16)],
        compiler_params=pltpu.CompilerParams(
            dimension_semantics=("arbitrary",), vmem_limit_bytes=VMEM_LIMIT_BYTES),
        name="merge_out_norm",
    )(x2, ya2, yb2, w_all, wa, wb, wo, lng_row, lnb_row)


def _lane_row(vec, offset):
    n = vec.shape[0]
    return jnp.pad(vec.astype(F32), (offset, LANES - offset - n)).reshape(1, LANES)


def _pack_w_in(w_in):
    o = _OFF_SWA
    groups = {
        "dn_qkv": w_in[:, _OFF_QKV:_OFF_DN_Z],
        "dn_z": w_in[:, _OFF_DN_Z:_OFF_DN_BA],
        "gates": w_in[:, _OFF_GATE:],
        "swa_q": w_in[:, o:o + SWA_WIDTH] * (SWA_HEAD_DIM ** -0.5),
        "swa_z": w_in[:, o + SWA_WIDTH + 2 * SWA_KV_WIDTH:_OFF_GATE],
        "swa_kv": w_in[:, o + SWA_WIDTH:o + SWA_WIDTH + 2 * SWA_KV_WIDTH],
        "dn_ba": jnp.pad(w_in[:, _OFF_DN_BA:_OFF_SWA], ((0, 0), (0, LANES - 2 * DN_HEADS))),
    }
    assert list(groups) == list(_W_ALL_LAYOUT)
    return jnp.concatenate([g.astype(BF16) for g in groups.values()], axis=1)


def _layer(x, w_in, conv_w, a_log, dt_bias, dn_norm_w, sinks, w_branch, w_out, ln_g, ln_b, alpha):
    b, s, d = x.shape
    w_all = _pack_w_in(w_in)
    y_a = _deltanet_branch(x, w_all, conv_w.astype(F32), _lane_row(a_log, DN_HEADS),
                           _lane_row(dt_bias, DN_HEADS), dn_norm_w.astype(F32).reshape(1, DN_HEAD_DIM))
    y_b = _swa_branch(x, sinks.astype(F32), w_all)
    w_ab = w_branch.astype(BF16)
    out = _merge(x.reshape(b * s, d), y_a.reshape(b * s, DN_WIDTH), y_b.reshape(b * s, SWA_WIDTH),
                 w_all, w_ab[0], w_ab[1], w_out.astype(BF16),
                 ln_g.astype(F32).reshape(1, d), ln_b.astype(F32).reshape(1, d), alpha)
    return out.reshape(b, s, d)


def kernel(x, w_in, conv_w, a_log, dt_bias, dn_norm_w, sinks, w_branch, w_out, ln_g, ln_b):
    depth = w_in.shape[0]
    assert w_in.shape[1:] == (D_MODEL, _OFF_GATE + 2 * D_MODEL) and conv_w.shape[1:] == (CONV_WIDTH, 3 * DN_WIDTH)
    alpha = (2.0 * depth) ** 0.25
    for layer in range(depth):
        x = _layer(x, w_in[layer], conv_w[layer], a_log[layer], dt_bias[layer], dn_norm_w[layer],
                   sinks[layer], w_branch[layer], w_out[layer], ln_g[layer], ln_b[layer], alpha)
    return x
```

```python
import functools

import jax
import jax.numpy as jnp
import numpy as np
from jax import lax
from jax.experimental import pallas as pl
from jax.experimental.pallas import tpu as pltpu

F32 = jnp.float32
BF16 = jnp.bfloat16

D_MODEL = 1024
DN_HEADS = 4
DN_HEAD_DIM = 128
DN_WIDTH = DN_HEADS * DN_HEAD_DIM
CONV_WIDTH = 4
CHUNK = 64
SWA_Q_HEADS = 8
SWA_KV_HEADS = 2
SWA_HEAD_DIM = 64
SWA_WIDTH = SWA_Q_HEADS * SWA_HEAD_DIM
SWA_KV_WIDTH = SWA_KV_HEADS * SWA_HEAD_DIM
WINDOW = 128
ROPE_THETA = 500000.0
ROPE_DIM = SWA_HEAD_DIM // 4
LN_EPS = 1e-5
NORM_EPS = 1e-6
MASK_VALUE = -1e30

LANES = 128
SUBLANES = 8
VMEM_LIMIT_BYTES = 48 * 1024 * 1024

DN_TILE = 1024
DN_SUB = 256
DN_PROJ_COLS = 256
SWA_TILE = 1024
SWA_SUB = 256
SWA_GROUPS_IN_FLIGHT = 2
MERGE_TILE = 1024
MERGE_SUB = 256
assert CONV_WIDTH == 4

_OFF_QKV = 0
_OFF_DN_Z = 3 * DN_WIDTH
_OFF_DN_BA = _OFF_DN_Z + DN_WIDTH
_OFF_SWA = _OFF_DN_BA + 2 * DN_HEADS
_SWA_COLS = SWA_WIDTH + 2 * SWA_KV_WIDTH + SWA_WIDTH
_OFF_GATE = _OFF_SWA + _SWA_COLS


def _make_w_all_layout():
    widths = (("dn_qkv", 3 * DN_WIDTH), ("dn_z", DN_WIDTH), ("gates", 2 * D_MODEL), ("swa_q", SWA_WIDTH),
              ("swa_z", SWA_WIDTH), ("swa_kv", 2 * SWA_KV_WIDTH), ("dn_ba", LANES))
    layout, off = {}, 0
    for name, width in widths:
        assert off % width == 0
        layout[name] = (off, width)
        off += width
    return layout


_W_ALL_LAYOUT = _make_w_all_layout()


def _mm(a, b):
    return jnp.dot(a.astype(BF16), b.astype(BF16), preferred_element_type=F32)


def _mm_nt(a, b):
    return lax.dot_general(a.astype(BF16), b.astype(BF16), (((1,), (1,)), ((), ())),
                           preferred_element_type=F32)


def _mm_tn(a, b):
    return lax.dot_general(a.astype(BF16), b.astype(BF16), (((0,), (0,)), ((), ())),
                           preferred_element_type=F32)


def _silu(v):
    return v * jax.nn.sigmoid(v)


def _interleave(streams):
    live = [[gen, 0, steps] for gen, steps in streams]
    while live:
        entry = min(live, key=lambda e: e[1] / e[2])
        try:
            next(entry[0])
            entry[1] += 1
        except StopIteration:
            live.remove(entry)


def _dn_kernel(x_ref, wqkv_ref, wz_ref, wba_ref, convw_ref, alog_ref, dtb_ref, normw_ref, y_ref,
               xb_s, hbuf, q_s, k_s, v_s, z_s, beta_s, gc_s, gct_s, u_s, w_s, qd_s, kt_s, a_s, state):
    t = pl.program_id(1)
    tile = x_ref.shape[0]
    sub = DN_SUB
    n_sub = tile // sub
    halo = SUBLANES
    pair = 2 * CHUNK
    pairs_per_sub = sub // pair
    heads = range(DN_HEADS)

    @pl.when(t == 0)
    def _():
        hbuf[0:halo, :] = jnp.zeros((halo, 3 * DN_WIDTH), F32)
        state[...] = jnp.zeros_like(state)

    row = lax.broadcasted_iota(jnp.int32, (CHUNK, pair), 0)
    lane = lax.broadcasted_iota(jnp.int32, (CHUNK, pair), 1)
    col = lane % CHUNK
    left = lane < CHUNK
    causal = row >= col
    strict = row > col
    xor_ij = row ^ col
    eye = jnp.where(row == col, 1.0, 0.0).astype(F32)
    first_chunk = lax.broadcasted_iota(jnp.int32, (pair, LANES), 0) < CHUNK
    pos = lax.broadcasted_iota(jnp.int32, (sub, LANES), 0) % CHUNK

    def head_cols(h):
        return slice(h * DN_HEAD_DIM, (h + 1) * DN_HEAD_DIM)

    def side_by_side(m):
        return jnp.where(left, m[:CHUNK], m[CHUNK:])

    def block_diag(m):
        zero = jnp.zeros_like(m)
        return jnp.concatenate([jnp.where(left, m, zero), jnp.where(left, zero, m)], axis=0)

    def front(j):
        r0 = j * sub
        rows = slice(r0, r0 + sub)
        xb_s[rows, :] = x_ref[rows, :].astype(BF16)
        ba = jnp.dot(xb_s[rows, :], wba_ref[...], preferred_element_type=F32)
        yield
        beta_s[rows, :] = jax.nn.sigmoid(ba)
        xg = ba + dtb_ref[...]
        softplus = jnp.maximum(xg, 0.0) + jnp.log1p(jnp.exp(-jnp.abs(xg)))
        gc = -jnp.exp(alog_ref[...]) * softplus
        step = 1
        while step < CHUNK:
            gc = gc + jnp.where(pos >= step, pltpu.roll(gc, step, 0), 0.0)
            step *= 2
        gc_s[rows, :] = gc
        gct_s[:, rows] = gc.T
        yield
        dests = (q_s, k_s, v_s)
        slabs_per_chunk = DN_PROJ_COLS // DN_HEAD_DIM
        for s in range(3 * DN_HEADS):
            if s % slabs_per_chunk == 0:
                chunk = slice(s * DN_HEAD_DIM, s * DN_HEAD_DIM + DN_PROJ_COLS)
                hbuf[halo + r0:halo + r0 + sub, chunk] = jnp.dot(
                    xb_s[rows, :], wqkv_ref[:, chunk], preferred_element_type=F32)
                z0 = (s // slabs_per_chunk) * DN_PROJ_COLS
                if z0 < DN_WIDTH:
                    z_s[rows, z0:z0 + DN_PROJ_COLS] = jnp.dot(
                        xb_s[rows, :], wz_ref[:, z0:z0 + DN_PROJ_COLS], preferred_element_type=F32)
                yield
            cols = slice(s * DN_HEAD_DIM, (s + 1) * DN_HEAD_DIM)
            ext = hbuf[r0:r0 + halo + sub, cols]
            prev = pltpu.roll(ext, 1, 0)
            older = convw_ref[0:1, cols] * prev + convw_ref[1:2, cols] * ext
            newer = convw_ref[2:3, cols] * prev + convw_ref[3:4, cols] * ext
            acc = (pltpu.roll(older, 2, 0) + newer)[halo:, :]
            yield
            a = _silu(acc)
            if s < 2 * DN_HEADS:
                a = a * lax.rsqrt(jnp.sum(a * a, axis=-1, keepdims=True) + NORM_EPS)
                if s < DN_HEADS:
                    a = a * (DN_HEAD_DIM ** -0.5)
            dests[s // DN_HEADS][rows, head_cols(s % DN_HEADS)] = a
            yield
        if j == n_sub - 1:
            hbuf[0:halo, :] = hbuf[tile:tile + halo, :]

    front_steps = 2 + 3 * DN_WIDTH // DN_PROJ_COLS + 2 * 3 * DN_HEADS

    def factor(j):
        problems = [(j * sub + p * pair, h) for p in range(pairs_per_sub) for h in heads]
        lows, rhs = [], []
        for r0, h in problems:
            rows = slice(r0, r0 + pair)
            la = DN_HEADS + h
            if h == 0:
                gc_p = gc_s[rows, :]
                e_gc = jnp.exp(gc_p)
                g_end = jnp.where(first_chunk, gc_p[CHUNK - 1:CHUNK, :], gc_p[pair - 1:pair, :])
                e_tail = jnp.exp(g_end - gc_p)
                beta_p = beta_s[rows, :]
            g_col = gc_p[:, la:la + 1]
            g_row = gct_s[la:la + 1, rows]
            b_col = beta_p[:, h:h + 1]
            eg_col = e_gc[:, la:la + 1]
            qh = q_s[rows, head_cols(h)]
            kh = k_s[rows, head_cols(h)]
            vh = v_s[rows, head_cols(h)]
            g_diff = jnp.where(left, g_col[:CHUNK], g_col[CHUNK:]) - g_row
            decay = jnp.where(causal, jnp.exp(jnp.where(causal, g_diff, 0.0)), 0.0)
            kb = kh * b_col
            kq = _mm_nt(jnp.concatenate([kb, qh], axis=0), kh)
            lows.append(jnp.where(strict, side_by_side(kq[:pair]) * decay, 0.0))
            a_s[rows, head_cols(h)] = block_diag((side_by_side(kq[pair:]) * decay).astype(BF16))
            rhs.append(jnp.concatenate([vh * b_col, kb * eg_col], axis=1).astype(BF16))
            qd_s[rows, head_cols(h)] = (qh * eg_col).astype(BF16)
            kt_s[rows, head_cols(h)] = (kh * e_tail[:, la:la + 1]).astype(BF16)
            yield
        invs = [eye - jnp.where(xor_ij == 1, low, 0.0) for low in lows]
        level = 1
        while (1 << level) < CHUNK:
            joins = (xor_ij >> level) == 1
            cs = [block_diag(jnp.where(joins, low, 0.0).astype(BF16)) for low in lows]
            xs = [inv.astype(BF16) for inv in invs]
            xcs = []
            for x, c in zip(xs, cs):
                xcs.append(_mm(x, c))
                yield
            for i, (xc, x) in enumerate(zip(xcs, xs)):
                invs[i] = invs[i] - _mm(xc, block_diag(x))
                yield
            level += 1
        for (r0, h), t_inv, r in zip(problems, invs, rhs):
            rows = slice(r0, r0 + pair)
            uw = _mm(block_diag(t_inv.astype(BF16)), r)
            u_s[rows, head_cols(h)] = uw[:, :DN_HEAD_DIM]
            w_s[rows, head_cols(h)] = uw[:, DN_HEAD_DIM:].astype(BF16)
            yield

    levels = CHUNK.bit_length() - 2
    factor_steps = (2 + 2 * levels) * pairs_per_sub * DN_HEADS

    def recur(j):
        for p in range(pairs_per_sub):
            p0 = j * sub + p * pair
            v_new = [[None, None] for _ in heads]
            q_state = [[None, None] for _ in heads]
            for cc in range(2):
                r0 = p0 + cc * CHUNK
                rows = slice(r0, r0 + CHUNK)
                e_end = jnp.exp(gc_s[r0 + CHUNK - 1:r0 + CHUNK, :])
                s_in = [state[h] for h in heads]
                prods = []
                for h in heads:
                    lhs = jnp.concatenate([w_s[rows, head_cols(h)], qd_s[rows, head_cols(h)]], axis=0)
                    prods.append(_mm(lhs, s_in[h]))
                for h in heads:
                    v_new[h][cc] = u_s[rows, head_cols(h)] - prods[h][:CHUNK]
                    q_state[h][cc] = prods[h][CHUNK:]
                yield
                for h in heads:
                    la = DN_HEADS + h
                    state[h] = (s_in[h] * e_end[:, la:la + 1]
                                + _mm_tn(kt_s[rows, head_cols(h)], v_new[h][cc]))
                yield
            rows = slice(p0, p0 + pair)
            for h in heads:
                o = (jnp.concatenate(q_state[h], axis=0)
                     + _mm(a_s[rows, head_cols(h)], jnp.concatenate(v_new[h], axis=0)))
                o = o * lax.rsqrt(jnp.mean(o * o, axis=-1, keepdims=True) + NORM_EPS) * normw_ref[...]
                y_ref[rows, head_cols(h)] = (o * _silu(z_s[rows, head_cols(h)])).astype(y_ref.dtype)
                yield

    recur_steps = (4 + DN_HEADS) * pairs_per_sub

    stages = ((front, front_steps), (factor, factor_steps), (recur, recur_steps))
    for slot in range(n_sub + len(stages) - 1):
        _interleave([(stage(slot - k), steps) for k, (stage, steps) in enumerate(stages)
                     if 0 <= slot - k < n_sub])


def _w_all_spec(name, rows):
    off, width = _W_ALL_LAYOUT[name]
    return pl.BlockSpec((rows, width), lambda *_: (0, off // width))


def _deltanet_branch(x, w_all, conv_w, alog_row, dtb_row, normw_row):
    b, s, d = x.shape
    tile = DN_TILE
    assert d == D_MODEL and s % tile == 0 and tile % DN_SUB == 0 and DN_SUB % (2 * CHUNK) == 0
    assert (3 * DN_WIDTH) % DN_PROJ_COLS == 0 and DN_PROJ_COLS % DN_HEAD_DIM == 0
    const = lambda bi, ti: (0, 0)
    return pl.pallas_call(
        _dn_kernel,
        grid=(b, s // tile),
        in_specs=[
            pl.BlockSpec((None, tile, d), lambda bi, ti: (bi, ti, 0)),
            _w_all_spec("dn_qkv", d),
            _w_all_spec("dn_z", d),
            _w_all_spec("dn_ba", d),
            pl.BlockSpec(conv_w.shape, const),
            pl.BlockSpec(alog_row.shape, const),
            pl.BlockSpec(dtb_row.shape, const),
            pl.BlockSpec(normw_row.shape, const),
        ],
        out_specs=pl.BlockSpec((None, tile, DN_WIDTH), lambda bi, ti: (bi, ti, 0)),
        out_shape=jax.ShapeDtypeStruct((b, s, DN_WIDTH), BF16),
        scratch_shapes=[
            pltpu.VMEM((tile, d), BF16),
            pltpu.VMEM((tile + SUBLANES, 3 * DN_WIDTH), F32),
            pltpu.VMEM((tile, DN_WIDTH), F32),
            pltpu.VMEM((tile, DN_WIDTH), F32),
            pltpu.VMEM((tile, DN_WIDTH), F32),
            pltpu.VMEM((tile, DN_WIDTH), F32),
            pltpu.VMEM((tile, LANES), F32),
            pltpu.VMEM((tile, LANES), F32),
            pltpu.VMEM((LANES, tile), F32),
            pltpu.VMEM((tile, DN_WIDTH), F32),
            pltpu.VMEM((tile, DN_WIDTH), BF16),
            pltpu.VMEM((tile, DN_WIDTH), BF16),
            pltpu.VMEM((tile, DN_WIDTH), BF16),
            pltpu.VMEM((tile, DN_WIDTH), BF16),
            pltpu.VMEM((DN_HEADS, DN_HEAD_DIM, DN_HEAD_DIM), F32),
        ],
        compiler_params=pltpu.CompilerParams(
            dimension_semantics=("arbitrary", "arbitrary"), vmem_limit_bytes=VMEM_LIMIT_BYTES),
        name="deltanet_branch",
    )(x, w_all, w_all, w_all, conv_w, alog_row, dtb_row, normw_row)


def _swa_kernel(sinks_ref, x_ref, wq_ref, wkv_ref, wz_ref, rope_off_ref, rope_start_ref, y_ref,
                kband, vband, q_s, z_s):
    t = pl.program_id(1)
    w = WINDOW
    tile = x_ref.shape[0]

    @pl.when(t == 0)
    def _():
        kband[:, 0:w, :] = jnp.zeros((2 * SWA_KV_HEADS, w, LANES), BF16)
        vband[:, 0:w, :] = jnp.zeros((2 * SWA_KV_HEADS, w, LANES), BF16)

    half = ROPE_DIM // 2
    sub = SWA_SUB
    n_sub = tile // sub
    n_pairs = SWA_Q_HEADS // 2
    group = SWA_Q_HEADS // SWA_KV_HEADS
    lane = lax.broadcasted_iota(jnp.int32, (sub, LANES), 1)
    lo = lane < SWA_HEAD_DIM
    head_lane = lane % SWA_HEAD_DIM

    qi = lax.broadcasted_iota(jnp.int32, (w, 2 * w), 0)
    kj = lax.broadcasted_iota(jnp.int32, (w, 2 * w), 1)
    in_band = (kj > qi) & (kj <= qi + w)
    bias = jnp.where(in_band, 0.0, MASK_VALUE).astype(F32)
    bias_first = jnp.where(in_band & (kj >= jnp.where(t == 0, w, 0)), 0.0, MASK_VALUE).astype(F32)

    def front(j):
        r0 = j * sub
        rows = slice(r0, r0 + sub)
        cos_o = rope_off_ref[rows, 0:LANES]
        sin_o = rope_off_ref[rows, LANES:2 * LANES]
        cos_s = rope_start_ref[:, 0:LANES]
        sin_s = rope_start_ref[:, LANES:2 * LANES]
        cos_p = cos_s * cos_o - sin_s * sin_o
        sin_p = sin_s * cos_o + cos_s * sin_o
        sin_a = jnp.where(head_lane < half, -sin_p, 0.0)
        sin_b = jnp.where(head_lane >= half, sin_p, 0.0)

        def rope(v):
            return v * cos_p + pltpu.roll(v, LANES - half, 1) * sin_a + pltpu.roll(v, half, 1) * sin_b

        xb = x_ref[rows, :].astype(BF16)
        q = jnp.dot(xb, wq_ref[...], preferred_element_type=F32)
        yield
        for pair in range(n_pairs):
            p0 = pair * LANES
            q_s[rows, p0:p0 + LANES] = rope(q[:, p0:p0 + LANES]).astype(BF16)
        yield
        kv = jnp.dot(xb, wkv_ref[...], preferred_element_type=F32)
        z_s[rows, :] = jnp.dot(xb, wz_ref[...], preferred_element_type=F32)
        yield
        k = rope(kv[:, :LANES])
        v = kv[:, LANES:]
        band_rows = slice(w + r0, w + r0 + sub)
        for src, band in ((k, kband), (v, vband)):
            swapped = pltpu.roll(src, SWA_HEAD_DIM, 1)
            band[0, band_rows, :] = jnp.where(lo, src, 0.0).astype(BF16)
            band[1, band_rows, :] = jnp.where(lo, 0.0, swapped).astype(BF16)
            band[2, band_rows, :] = jnp.where(lo, swapped, 0.0).astype(BF16)
            band[3, band_rows, :] = jnp.where(lo, 0.0, src).astype(BF16)
        yield

    front_steps = 4

    slabs_per_group = group // 2
    upper_rows = lax.broadcasted_iota(jnp.int32, (slabs_per_group * w, 1), 0) >= w

    def kv_group(r0, g, blk_bias):
        slabs = [slice((slabs_per_group * g + i) * LANES, (slabs_per_group * g + i + 1) * LANES)
                 for i in range(slabs_per_group)]
        q_rows = jnp.concatenate([q_s[r0:r0 + w, cols] for cols in slabs], axis=0)
        bias_rows = jnp.concatenate([blk_bias] * slabs_per_group, axis=0)
        scores, sinks = [], []
        for hf in range(2):
            heads_hf = [group * g + 2 * i + hf for i in range(slabs_per_group)]
            sinks.append(jnp.where(upper_rows, sinks_ref[heads_hf[1]], sinks_ref[heads_hf[0]]))
            scores.append(_mm_nt(q_rows, kband[2 * g + hf, r0:r0 + 2 * w, :]) + bias_rows)
        yield
        probs, denoms = [], []
        for s, sink in zip(scores, sinks):
            m = jnp.maximum(jnp.max(jnp.maximum(s[:, :w], s[:, w:]), axis=-1, keepdims=True), sink)
            p = jnp.exp(s - m)
            denoms.append(jnp.sum(p[:, :w] + p[:, w:], axis=-1, keepdims=True) + jnp.exp(sink - m))
            probs.append(p.astype(BF16))
        yield
        acc = None
        for hf in range(2):
            o = _mm(probs[hf], vband[2 * g + hf, r0:r0 + 2 * w, :]) / denoms[hf]
            acc = o if acc is None else acc + o
        for i, cols in enumerate(slabs):
            y_ref[r0:r0 + w, cols] = (
                acc[i * w:(i + 1) * w] * _silu(z_s[r0:r0 + w, cols])).astype(y_ref.dtype)
        yield

    def attend(j):
        items = [(j * sub + b * w, g) for b in range(sub // w) for g in range(SWA_KV_HEADS)]
        for g0 in range(0, len(items), SWA_GROUPS_IN_FLIGHT):
            streams = [kv_group(r0, g, bias_first if r0 == 0 else bias)
                       for r0, g in items[g0:g0 + SWA_GROUPS_IN_FLIGHT]]
            for _ in range(3):
                for stream in streams:
                    next(stream)
                yield

    attend_steps = 3 * (sub // w) * SWA_KV_HEADS // SWA_GROUPS_IN_FLIGHT

    stages = ((front, front_steps), (attend, attend_steps))
    for slot in range(n_sub + len(stages) - 1):
        _interleave([(stage(slot - k), steps) for k, (stage, steps) in enumerate(stages)
                     if 0 <= slot - k < n_sub])

    kband[:, 0:w, :] = kband[:, tile:tile + w, :]
    vband[:, 0:w, :] = vband[:, tile:tile + w, :]


def _swa_branch(x, sinks, w_all):
    b, s, d = x.shape
    w = WINDOW
    tile = SWA_TILE
    assert d == D_MODEL and s % tile == 0 and tile % SWA_SUB == 0 and SWA_SUB % w == 0
    assert SWA_Q_HEADS // SWA_KV_HEADS == 4 and 2 * SWA_HEAD_DIM == LANES
    assert (SWA_SUB // w * SWA_KV_HEADS) % SWA_GROUPS_IN_FLIGHT == 0
    rope_offsets, rope_starts = _rope_tables(s, tile)
    return pl.pallas_call(
        _swa_kernel,
        grid=(b, s // tile),
        in_specs=[
            pl.BlockSpec(memory_space=pltpu.SMEM),
            pl.BlockSpec((None, tile, d), lambda bi, ti: (bi, ti, 0)),
            _w_all_spec("swa_q", d),
            _w_all_spec("swa_kv", d),
            _w_all_spec("swa_z", d),
            pl.BlockSpec((tile, 2 * LANES), lambda bi, ti: (0, 0)),
            pl.BlockSpec((None, 1, 2 * LANES), lambda bi, ti: (ti, 0, 0)),
        ],
        out_specs=pl.BlockSpec((None, tile, SWA_WIDTH), lambda bi, ti: (bi, ti, 0)),
        out_shape=jax.ShapeDtypeStruct((b, s, SWA_WIDTH), BF16),
        scratch_shapes=[
            pltpu.VMEM((2 * SWA_KV_HEADS, w + tile, LANES), BF16),
            pltpu.VMEM((2 * SWA_KV_HEADS, w + tile, LANES), BF16),
            pltpu.VMEM((tile, SWA_WIDTH), BF16),
            pltpu.VMEM((tile, SWA_WIDTH), F32),
        ],
        compiler_params=pltpu.CompilerParams(
            dimension_semantics=("arbitrary", "arbitrary"), vmem_limit_bytes=VMEM_LIMIT_BYTES),
        name="swa_branch",
    )(sinks, x, w_all, w_all, w_all, rope_offsets, rope_starts)


def _rope_tables(seq, tile):
    inv_freq = ROPE_THETA ** (-np.arange(0, ROPE_DIM, 2, dtype=np.float64) / ROPE_DIM)

    def lane_pattern(positions):
        ang = positions[:, None] * inv_freq[None, :]
        n = positions.shape[0]
        rest = SWA_HEAD_DIM - ROPE_DIM
        reps = LANES // SWA_HEAD_DIM
        cos = np.concatenate([np.cos(ang)] * 2 + [np.ones((n, rest))], axis=1)
        sin = np.concatenate([np.sin(ang)] * 2 + [np.zeros((n, rest))], axis=1)
        return np.concatenate([np.tile(cos, (1, reps)), np.tile(sin, (1, reps))], axis=1).astype(np.float32)

    offsets = lane_pattern(np.arange(tile, dtype=np.float64))
    starts = lane_pattern(np.arange(seq // tile, dtype=np.float64) * tile)
    return jnp.asarray(offsets), jnp.asarray(starts.reshape(seq // tile, 1, 2 * LANES))


def _merge_kernel(alpha, x_ref, ya_ref, yb_ref, wg_ref, wa_ref, wb_ref, wo_ref, lng_ref, lnb_ref, o_ref,
                  merged_s):
    tile = x_ref.shape[0]
    sub = MERGE_SUB
    n_sub = tile // sub

    def gated_merge(j):
        rows = slice(j * sub, (j + 1) * sub)
        xb = x_ref[rows, :].astype(BF16)
        gate_a = jax.nn.sigmoid(jnp.dot(xb, wg_ref[:, :D_MODEL], preferred_element_type=F32))
        pa = jnp.dot(ya_ref[rows, :], wa_ref[...], preferred_element_type=F32)
        yield
        gate_b = jax.nn.sigmoid(jnp.dot(xb, wg_ref[:, D_MODEL:], preferred_element_type=F32))
        pb = jnp.dot(yb_ref[rows, :], wb_ref[...], preferred_element_type=F32)
        yield
        merged_s[rows, :] = (gate_a * pa + gate_b * pb).astype(BF16)
        yield

    def project_norm(j):
        rows = slice(j * sub, (j + 1) * sub)
        out = jnp.dot(merged_s[rows, :], wo_ref[...], preferred_element_type=F32)
        yield
        r = alpha * x_ref[rows, :] + out
        mu = jnp.mean(r, axis=-1, keepdims=True)
        cen = r - mu
        var = jnp.mean(cen * cen, axis=-1, keepdims=True)
        o_ref[rows, :] = (cen * lax.rsqrt(var + LN_EPS) * lng_ref[...] + lnb_ref[...]).astype(o_ref.dtype)
        yield

    stages = ((gated_merge, 3), (project_norm, 2))
    for slot in range(n_sub + len(stages) - 1):
        _interleave([(stage(slot - k), steps) for k, (stage, steps) in enumerate(stages)
                     if 0 <= slot - k < n_sub])


def _merge(x2, ya2, yb2, w_all, wa, wb, wo, lng_row, lnb_row, alpha):
    n, d = x2.shape
    tile = MERGE_TILE
    assert d == D_MODEL and n % tile == 0 and tile % MERGE_SUB == 0
    const = lambda i: (0, 0)
    row = lambda i: (i, 0)
    return pl.pallas_call(
        functools.partial(_merge_kernel, alpha),
        grid=(n // tile,),
        in_specs=[
            pl.BlockSpec((tile, d), row),
            pl.BlockSpec((tile, ya2.shape[1]), row),
            pl.BlockSpec((tile, yb2.shape[1]), row),
            _w_all_spec("gates", d),
            pl.BlockSpec(wa.shape, const),
            pl.BlockSpec(wb.shape, const),
            pl.BlockSpec(wo.shape, const),
            pl.BlockSpec(lng_row.shape, const),
            pl.BlockSpec(lnb_row.shape, const),
        ],
        out_specs=pl.BlockSpec((tile, d), row),
        out_shape=jax.ShapeDtypeStruct((n, d), x2.dtype),
        scratch_shapes=[pltpu.VMEM((tile, d), BF16)],
        compiler_params=pltpu.CompilerParams(
            dimension_semantics=("arbitrary",), vmem_limit_bytes=VMEM_LIMIT_BYTES),
        name="merge_out_norm",
    )(x2, ya2, yb2, w_all, wa, wb, wo, lng_row, lnb_row)


def _lane_row(vec, offset):
    n = vec.shape[0]
    return jnp.pad(vec.astype(F32), (offset, LANES - offset - n)).reshape(1, LANES)


def _pack_w_in(w_in):
    o = _OFF_SWA
    groups = {
        "dn_qkv": w_in[:, _OFF_QKV:_OFF_DN_Z],
        "dn_z": w_in[:, _OFF_DN_Z:_OFF_DN_BA],
        "gates": w_in[:, _OFF_GATE:],
        "swa_q": w_in[:, o:o + SWA_WIDTH] * (SWA_HEAD_DIM ** -0.5),
        "swa_z": w_in[:, o + SWA_WIDTH + 2 * SWA_KV_WIDTH:_OFF_GATE],
        "swa_kv": w_in[:, o + SWA_WIDTH:o + SWA_WIDTH + 2 * SWA_KV_WIDTH],
        "dn_ba": jnp.pad(w_in[:, _OFF_DN_BA:_OFF_SWA], ((0, 0), (0, LANES - 2 * DN_HEADS))),
    }
    assert list(groups) == list(_W_ALL_LAYOUT)
    return jnp.concatenate([g.astype(BF16) for g in groups.values()], axis=1)


def _layer(x, w_in, conv_w, a_log, dt_bias, dn_norm_w, sinks, w_branch, w_out, ln_g, ln_b, alpha):
    b, s, d = x.shape
    w_all = _pack_w_in(w_in)
    y_a = _deltanet_branch(x, w_all, conv_w.astype(F32), _lane_row(a_log, DN_HEADS),
                           _lane_row(dt_bias, DN_HEADS), dn_norm_w.astype(F32).reshape(1, DN_HEAD_DIM))
    y_b = _swa_branch(x, sinks.astype(F32), w_all)
    w_ab = w_branch.astype(BF16)
    out = _merge(x.reshape(b * s, d), y_a.reshape(b * s, DN_WIDTH), y_b.reshape(b * s, SWA_WIDTH),
                 w_all, w_ab[0], w_ab[1], w_out.astype(BF16),
                 ln_g.astype(F32).reshape(1, d), ln_b.astype(F32).reshape(1, d), alpha)
    return out.reshape(b, s, d)


def kernel(x, w_in, conv_w, a_log, dt_bias, dn_norm_w, sinks, w_branch, w_out, ln_g, ln_b):
    depth = w_in.shape[0]
    assert w_in.shape[1:] == (D_MODEL, _OFF_GATE + 2 * D_MODEL) and conv_w.shape[1:] == (CONV_WIDTH, 3 * DN_WIDTH)
    alpha = (2.0 * depth) ** 0.25
    for layer in range(depth):
        x = _layer(x, w_in[layer], conv_w[layer], a_log[layer], dt_bias[layer], dn_norm_w[layer],
                   sinks[layer], w_branch[layer], w_out[layer], ln_g[layer], ln_b[layer], alpha)
    return x
```

```python
import functools

import jax
import jax.numpy as jnp
from jax import lax
from jax.experimental import pallas as pl
from jax.experimental.pallas import tpu as pltpu

F32 = jnp.float32
BF16 = jnp.bfloat16

D_MODEL = 1024
DN_HEADS = 4
DN_HEAD_DIM = 128
DN_WIDTH = DN_HEADS * DN_HEAD_DIM
CONV_WIDTH = 4
CHUNK = 64
SWA_Q_HEADS = 8
SWA_KV_HEADS = 2
SWA_HEAD_DIM = 64
SWA_WIDTH = SWA_Q_HEADS * SWA_HEAD_DIM
SWA_KV_WIDTH = SWA_KV_HEADS * SWA_HEAD_DIM
WINDOW = 128
ROPE_THETA = 500000.0
ROPE_DIM = SWA_HEAD_DIM // 4
LN_EPS = 1e-5
NORM_EPS = 1e-6
MASK_VALUE = -1e30

LANES = 128
SUBLANES = 8
VMEM_LIMIT_BYTES = 48 * 1024 * 1024

DN_TILE = 1024
DN_SUB = 256
DN_PROJ_COLS = 256
SWA_TILE = 1024
SWA_SUB = 256
SWA_GROUPS_IN_FLIGHT = 2
GATE_COLS = 512
MERGE_TILE = 1024
MERGE_SUB = 256
assert CONV_WIDTH == 4

_OFF_QKV = 0
_OFF_DN_Z = 3 * DN_WIDTH
_OFF_DN_BA = _OFF_DN_Z + DN_WIDTH
_OFF_SWA = _OFF_DN_BA + 2 * DN_HEADS
_SWA_COLS = SWA_WIDTH + 2 * SWA_KV_WIDTH + SWA_WIDTH
_OFF_GATE = _OFF_SWA + _SWA_COLS


def _make_w_all_layout():
    widths = (("dn_qkv", 3 * DN_WIDTH), ("dn_z", DN_WIDTH), ("gates", 2 * D_MODEL), ("swa_q", SWA_WIDTH),
              ("swa_z", SWA_WIDTH), ("swa_kv", 2 * SWA_KV_WIDTH), ("dn_ba", LANES))
    layout, off = {}, 0
    for name, width in widths:
        assert off % width == 0
        layout[name] = (off, width)
        off += width
    return layout


_W_ALL_LAYOUT = _make_w_all_layout()


def _mm(a, b):
    return jnp.dot(a.astype(BF16), b.astype(BF16), preferred_element_type=F32)


def _mm_nt(a, b):
    return lax.dot_general(a.astype(BF16), b.astype(BF16), (((1,), (1,)), ((), ())),
                           preferred_element_type=F32)


def _mm_tn(a, b):
    return lax.dot_general(a.astype(BF16), b.astype(BF16), (((0,), (0,)), ((), ())),
                           preferred_element_type=F32)


def _silu(v):
    return v * jax.nn.sigmoid(v)


def _interleave(streams):
    live = [[gen, 0, steps] for gen, steps in streams]
    while live:
        entry = min(live, key=lambda e: e[1] / e[2])
        try:
            next(entry[0])
            entry[1] += 1
        except StopIteration:
            live.remove(entry)


def _dn_kernel(x_ref, wqkv_ref, wz_ref, wba_ref, convw_ref, alog_ref, dtb_ref, normw_ref, y_ref,
               xb_s, hbuf, q_s, k_s, v_s, z_s, beta_s, gc_s, gct_s, u_s, w_s, qd_s, kt_s, a_s, state):
    t = pl.program_id(1)
    tile = x_ref.shape[0]
    sub = DN_SUB
    n_sub = tile // sub
    halo = SUBLANES
    pair = 2 * CHUNK
    pairs_per_sub = sub // pair
    heads = range(DN_HEADS)

    @pl.when(t == 0)
    def _():
        hbuf[0:halo, :] = jnp.zeros((halo, 3 * DN_WIDTH), F32)
        state[...] = jnp.zeros_like(state)

    row = lax.broadcasted_iota(jnp.int32, (CHUNK, pair), 0)
    lane = lax.broadcasted_iota(jnp.int32, (CHUNK, pair), 1)
    col = lane % CHUNK
    left = lane < CHUNK
    causal = row >= col
    strict = row > col
    xor_ij = row ^ col
    eye = jnp.where(row == col, 1.0, 0.0).astype(F32)
    first_chunk = lax.broadcasted_iota(jnp.int32, (pair, LANES), 0) < CHUNK
    pos = lax.broadcasted_iota(jnp.int32, (sub, LANES), 0) % CHUNK

    def head_cols(h):
        return slice(h * DN_HEAD_DIM, (h + 1) * DN_HEAD_DIM)

    def side_by_side(m):
        return jnp.where(left, m[:CHUNK], m[CHUNK:])

    def block_diag(m):
        zero = jnp.zeros_like(m)
        return jnp.concatenate([jnp.where(left, m, zero), jnp.where(left, zero, m)], axis=0)

    def front(j):
        r0 = j * sub
        rows = slice(r0, r0 + sub)
        xb_s[rows, :] = x_ref[rows, :].astype(BF16)
        ba = jnp.dot(xb_s[rows, :], wba_ref[...], preferred_element_type=F32)
        yield
        beta_s[rows, :] = jax.nn.sigmoid(ba)
        xg = ba + dtb_ref[...]
        softplus = jnp.maximum(xg, 0.0) + jnp.log1p(jnp.exp(-jnp.abs(xg)))
        gc = -jnp.exp(alog_ref[...]) * softplus
        step = 1
        while step < CHUNK:
            gc = gc + jnp.where(pos >= step, pltpu.roll(gc, step, 0), 0.0)
            step *= 2
        gc_s[rows, :] = gc
        gct_s[:, rows] = gc.T
        yield
        dests = (q_s, k_s, v_s)
        slabs_per_chunk = DN_PROJ_COLS // DN_HEAD_DIM
        for s in range(3 * DN_HEADS):
            if s % slabs_per_chunk == 0:
                chunk = slice(s * DN_HEAD_DIM, s * DN_HEAD_DIM + DN_PROJ_COLS)
                hbuf[halo + r0:halo + r0 + sub, chunk] = jnp.dot(
                    xb_s[rows, :], wqkv_ref[:, chunk], preferred_element_type=F32)
                z0 = (s // slabs_per_chunk) * DN_PROJ_COLS
                if z0 < DN_WIDTH:
                    z_s[rows, z0:z0 + DN_PROJ_COLS] = jnp.dot(
                        xb_s[rows, :], wz_ref[:, z0:z0 + DN_PROJ_COLS], preferred_element_type=F32)
                yield
            cols = slice(s * DN_HEAD_DIM, (s + 1) * DN_HEAD_DIM)
            ext = hbuf[r0:r0 + halo + sub, cols]
            prev = pltpu.roll(ext, 1, 0)
            older = convw_ref[0:1, cols] * prev + convw_ref[1:2, cols] * ext
            newer = convw_ref[2:3, cols] * prev + convw_ref[3:4, cols] * ext
            acc = (pltpu.roll(older, 2, 0) + newer)[halo:, :]
            yield
            a = _silu(acc)
            if s < 2 * DN_HEADS:
                a = a * lax.rsqrt(jnp.sum(a * a, axis=-1, keepdims=True) + NORM_EPS)
                if s < DN_HEADS:
                    a = a * (DN_HEAD_DIM ** -0.5)
            dests[s // DN_HEADS][rows, head_cols(s % DN_HEADS)] = a
            yield
        if j == n_sub - 1:
            hbuf[0:halo, :] = hbuf[tile:tile + halo, :]

    front_steps = 2 + 3 * DN_WIDTH // DN_PROJ_COLS + 2 * 3 * DN_HEADS

    def factor(j):
        problems = [(j * sub + p * pair, h) for p in range(pairs_per_sub) for h in heads]
        lows, rhs = [], []
        for r0, h in problems:
            rows = slice(r0, r0 + pair)
            la = DN_HEADS + h
            if h == 0:
                gc_p = gc_s[rows, :]
                e_gc = jnp.exp(gc_p)
                g_end = jnp.where(first_chunk, gc_p[CHUNK - 1:CHUNK, :], gc_p[pair - 1:pair, :])
                e_tail = jnp.exp(g_end - gc_p)
                beta_p = beta_s[rows, :]
            g_col = gc_p[:, la:la + 1]
            g_row = gct_s[la:la + 1, rows]
            b_col = beta_p[:, h:h + 1]
            eg_col = e_gc[:, la:la + 1]
            qh = q_s[rows, head_cols(h)]
            kh = k_s[rows, head_cols(h)]
            vh = v_s[rows, head_cols(h)]
            g_diff = jnp.where(left, g_col[:CHUNK], g_col[CHUNK:]) - g_row
            decay = jnp.where(causal, jnp.exp(jnp.where(causal, g_diff, 0.0)), 0.0)
            kb = kh * b_col
            kq = _mm_nt(jnp.concatenate([kb, qh], axis=0), kh)
            lows.append(jnp.where(strict, side_by_side(kq[:pair]) * decay, 0.0))
            a_s[rows, head_cols(h)] = block_diag((side_by_side(kq[pair:]) * decay).astype(BF16))
            rhs.append(jnp.concatenate([vh * b_col, kb * eg_col], axis=1).astype(BF16))
            qd_s[rows, head_cols(h)] = (qh * eg_col).astype(BF16)
            kt_s[rows, head_cols(h)] = (kh * e_tail[:, la:la + 1]).astype(BF16)
            yield
        invs = [eye - jnp.where(xor_ij == 1, low, 0.0) for low in lows]
        level = 1
        while (1 << level) < CHUNK:
            joins = (xor_ij >> level) == 1
            cs = [block_diag(jnp.where(joins, low, 0.0).astype(BF16)) for low in lows]
            xs = [inv.astype(BF16) for inv in invs]
            xcs = []
            for x, c in zip(xs, cs):
                xcs.append(_mm(x, c))
                yield
            for i, (xc, x) in enumerate(zip(xcs, xs)):
                invs[i] = invs[i] - _mm(xc, block_diag(x))
                yield
            level += 1
        for (r0, h), t_inv, r in zip(problems, invs, rhs):
            rows = slice(r0, r0 + pair)
            uw = _mm(block_diag(t_inv.astype(BF16)), r)
            u_s[rows, head_cols(h)] = uw[:, :DN_HEAD_DIM]
            w_s[rows, head_cols(h)] = uw[:, DN_HEAD_DIM:].astype(BF16)
            yield

    levels = CHUNK.bit_length() - 2
    factor_steps = (2 + 2 * levels) * pairs_per_sub * DN_HEADS

    def recur(j):
        for p in range(pairs_per_sub):
            p0 = j * sub + p * pair
            v_new = [[None, None] for _ in heads]
            q_state = [[None, None] for _ in heads]
            for cc in range(2):
                r0 = p0 + cc * CHUNK
                rows = slice(r0, r0 + CHUNK)
                e_end = jnp.exp(gc_s[r0 + CHUNK - 1:r0 + CHUNK, :])
                s_in = [state[h] for h in heads]
                prods = []
                for h in heads:
                    lhs = jnp.concatenate([w_s[rows, head_cols(h)], qd_s[rows, head_cols(h)]], axis=0)
                    prods.append(_mm(lhs, s_in[h]))
                for h in heads:
                    v_new[h][cc] = u_s[rows, head_cols(h)] - prods[h][:CHUNK]
                    q_state[h][cc] = prods[h][CHUNK:]
                yield
                for h in heads:
                    la = DN_HEADS + h
                    state[h] = (s_in[h] * e_end[:, la:la + 1]
                                + _mm_tn(kt_s[rows, head_cols(h)], v_new[h][cc]))
                yield
            rows = slice(p0, p0 + pair)
            for h in heads:
                o = (jnp.concatenate(q_state[h], axis=0)
                     + _mm(a_s[rows, head_cols(h)], jnp.concatenate(v_new[h], axis=0)))
                o = o * lax.rsqrt(jnp.mean(o * o, axis=-1, keepdims=True) + NORM_EPS) * normw_ref[...]
                y_ref[rows, head_cols(h)] = (o * _silu(z_s[rows, head_cols(h)])).astype(y_ref.dtype)
                yield

    recur_steps = (4 + DN_HEADS) * pairs_per_sub

    stages = ((front, front_steps), (factor, factor_steps), (recur, recur_steps))
    for slot in range(n_sub + len(stages) - 1):
        _interleave([(stage(slot - k), steps) for k, (stage, steps) in enumerate(stages)
                     if 0 <= slot - k < n_sub])


def _w_all_spec(name, rows):
    off, width = _W_ALL_LAYOUT[name]
    return pl.BlockSpec((rows, width), lambda *_: (0, off // width))


def _deltanet_branch(x, w_all, conv_w, alog_row, dtb_row, normw_row):
    b, s, d = x.shape
    tile = DN_TILE
    assert d == D_MODEL and s % tile == 0 and tile % DN_SUB == 0 and DN_SUB % (2 * CHUNK) == 0
    assert (3 * DN_WIDTH) % DN_PROJ_COLS == 0 and DN_PROJ_COLS % DN_HEAD_DIM == 0
    const = lambda bi, ti: (0, 0)
    return pl.pallas_call(
        _dn_kernel,
        grid=(b, s // tile),
        in_specs=[
            pl.BlockSpec((None, tile, d), lambda bi, ti: (bi, ti, 0)),
            _w_all_spec("dn_qkv", d),
            _w_all_spec("dn_z", d),
            _w_all_spec("dn_ba", d),
            pl.BlockSpec(conv_w.shape, const),
            pl.BlockSpec(alog_row.shape, const),
            pl.BlockSpec(dtb_row.shape, const),
            pl.BlockSpec(normw_row.shape, const),
        ],
        out_specs=pl.BlockSpec((None, tile, DN_WIDTH), lambda bi, ti: (bi, ti, 0)),
        out_shape=jax.ShapeDtypeStruct((b, s, DN_WIDTH), BF16),
        scratch_shapes=[
            pltpu.VMEM((tile, d), BF16),
            pltpu.VMEM((tile + SUBLANES, 3 * DN_WIDTH), F32),
            pltpu.VMEM((tile, DN_WIDTH), F32),
            pltpu.VMEM((tile, DN_WIDTH), F32),
            pltpu.VMEM((tile, DN_WIDTH), F32),
            pltpu.VMEM((tile, DN_WIDTH), F32),
            pltpu.VMEM((tile, LANES), F32),
            pltpu.VMEM((tile, LANES), F32),
            pltpu.VMEM((LANES, tile), F32),
            pltpu.VMEM((tile, DN_WIDTH), F32),
            pltpu.VMEM((tile, DN_WIDTH), BF16),
            pltpu.VMEM((tile, DN_WIDTH), BF16),
            pltpu.VMEM((tile, DN_WIDTH), BF16),
            pltpu.VMEM((tile, DN_WIDTH), BF16),
            pltpu.VMEM((DN_HEADS, DN_HEAD_DIM, DN_HEAD_DIM), F32),
        ],
        compiler_params=pltpu.CompilerParams(
            dimension_semantics=("arbitrary", "arbitrary"), vmem_limit_bytes=VMEM_LIMIT_BYTES),
        name="deltanet_branch",
    )(x, w_all, w_all, w_all, conv_w, alog_row, dtb_row, normw_row)


def _swa_kernel(sinks_ref, x_ref, wq_ref, wkv_ref, wz_ref, wg_ref, rope_off_ref, rope_start_ref, y_ref, g_ref,
                kband, vband, q_s, z_s):
    t = pl.program_id(1)
    w = WINDOW
    tile = x_ref.shape[0]

    @pl.when(t == 0)
    def _():
        kband[:, 0:w, :] = jnp.zeros((2 * SWA_KV_HEADS, w, LANES), BF16)
        vband[:, 0:w, :] = jnp.zeros((2 * SWA_KV_HEADS, w, LANES), BF16)

    half = ROPE_DIM // 2
    sub = SWA_SUB
    n_sub = tile // sub
    n_pairs = SWA_Q_HEADS // 2
    group = SWA_Q_HEADS // SWA_KV_HEADS
    lane = lax.broadcasted_iota(jnp.int32, (sub, LANES), 1)
    lo = lane < SWA_HEAD_DIM
    head_lane = lane % SWA_HEAD_DIM

    qi = lax.broadcasted_iota(jnp.int32, (w, 2 * w), 0)
    kj = lax.broadcasted_iota(jnp.int32, (w, 2 * w), 1)
    in_band = (kj > qi) & (kj <= qi + w)
    bias = jnp.where(in_band, 0.0, MASK_VALUE).astype(F32)
    bias_first = jnp.where(in_band & (kj >= jnp.where(t == 0, w, 0)), 0.0, MASK_VALUE).astype(F32)

    def front(j):
        r0 = j * sub
        rows = slice(r0, r0 + sub)
        cos_o = rope_off_ref[rows, 0:LANES]
        sin_o = rope_off_ref[rows, LANES:2 * LANES]
        cos_s = rope_start_ref[:, 0:LANES]
        sin_s = rope_start_ref[:, LANES:2 * LANES]
        cos_p = cos_s * cos_o - sin_s * sin_o
        sin_p = sin_s * cos_o + cos_s * sin_o
        sin_a = jnp.where(head_lane < half, -sin_p, 0.0)
        sin_b = jnp.where(head_lane >= half, sin_p, 0.0)

        def rope(v):
            return v * cos_p + pltpu.roll(v, LANES - half, 1) * sin_a + pltpu.roll(v, half, 1) * sin_b

        xb = x_ref[rows, :].astype(BF16)
        q = jnp.dot(xb, wq_ref[...], preferred_element_type=F32)
        yield
        for pair in range(n_pairs):
            p0 = pair * LANES
            q_s[rows, p0:p0 + LANES] = rope(q[:, p0:p0 + LANES]).astype(BF16)
        yield
        kv = jnp.dot(xb, wkv_ref[...], preferred_element_type=F32)
        z_s[rows, :] = jnp.dot(xb, wz_ref[...], preferred_element_type=F32)
        yield
        k = rope(kv[:, :LANES])
        v = kv[:, LANES:]
        band_rows = slice(w + r0, w + r0 + sub)
        for src, band in ((k, kband), (v, vband)):
            swapped = pltpu.roll(src, SWA_HEAD_DIM, 1)
            band[0, band_rows, :] = jnp.where(lo, src, 0.0).astype(BF16)
            band[1, band_rows, :] = jnp.where(lo, 0.0, swapped).astype(BF16)
            band[2, band_rows, :] = jnp.where(lo, swapped, 0.0).astype(BF16)
            band[3, band_rows, :] = jnp.where(lo, 0.0, src).astype(BF16)
        yield
        for c0 in range(0, 2 * D_MODEL, GATE_COLS):
            gates = jnp.dot(xb, wg_ref[:, c0:c0 + GATE_COLS], preferred_element_type=F32)
            g_ref[rows, c0:c0 + GATE_COLS] = jax.nn.sigmoid(gates).astype(g_ref.dtype)
            yield

    front_steps = 4 + 2 * D_MODEL // GATE_COLS

    slabs_per_group = group // 2
    upper_rows = lax.broadcasted_iota(jnp.int32, (slabs_per_group * w, 1), 0) >= w

    def kv_group(r0, g, blk_bias):
        slabs = [slice((slabs_per_group * g + i) * LANES, (slabs_per_group * g + i + 1) * LANES)
                 for i in range(slabs_per_group)]
        q_rows = jnp.concatenate([q_s[r0:r0 + w, cols] for cols in slabs], axis=0)
        bias_rows = jnp.concatenate([blk_bias] * slabs_per_group, axis=0)
        scores, sinks = [], []
        for hf in range(2):
            heads_hf = [group * g + 2 * i + hf for i in range(slabs_per_group)]
            sinks.append(jnp.where(upper_rows, sinks_ref[heads_hf[1]], sinks_ref[heads_hf[0]]))
            scores.append(_mm_nt(q_rows, kband[2 * g + hf, r0:r0 + 2 * w, :]) + bias_rows)
        yield
        probs, denoms = [], []
        for s, sink in zip(scores, sinks):
            m = jnp.maximum(jnp.max(jnp.maximum(s[:, :w], s[:, w:]), axis=-1, keepdims=True), sink)
            p = jnp.exp(s - m)
            denoms.append(jnp.sum(p[:, :w] + p[:, w:], axis=-1, keepdims=True) + jnp.exp(sink - m))
            probs.append(p.astype(BF16))
        yield
        acc = None
        for hf in range(2):
            o = _mm(probs[hf], vband[2 * g + hf, r0:r0 + 2 * w, :]) / denoms[hf]
            acc = o if acc is None else acc + o
        for i, cols in enumerate(slabs):
            y_ref[r0:r0 + w, cols] = (
                acc[i * w:(i + 1) * w] * _silu(z_s[r0:r0 + w, cols])).astype(y_ref.dtype)
        yield

    def attend(j):
        items = [(j * sub + b * w, g) for b in range(sub // w) for g in range(SWA_KV_HEADS)]
        for g0 in range(0, len(items), SWA_GROUPS_IN_FLIGHT):
            streams = [kv_group(r0, g, bias_first if r0 == 0 else bias)
                       for r0, g in items[g0:g0 + SWA_GROUPS_IN_FLIGHT]]
            for _ in range(3):
                for stream in streams:
                    next(stream)
                yield

    attend_steps = 3 * (sub // w) * SWA_KV_HEADS // SWA_GROUPS_IN_FLIGHT

    stages = ((front, front_steps), (attend, attend_steps))
    for slot in range(n_sub + len(stages) - 1):
        _interleave([(stage(slot - k), steps) for k, (stage, steps) in enumerate(stages)
                     if 0 <= slot - k < n_sub])

    kband[:, 0:w, :] = kband[:, tile:tile + w, :]
    vband[:, 0:w, :] = vband[:, tile:tile + w, :]


def _swa_branch(x, sinks, w_all):
    b, s, d = x.shape
    w = WINDOW
    tile = SWA_TILE
    assert d == D_MODEL and s % tile == 0 and tile % SWA_SUB == 0 and SWA_SUB % w == 0
    assert SWA_Q_HEADS // SWA_KV_HEADS == 4 and 2 * SWA_HEAD_DIM == LANES
    assert (SWA_SUB // w * SWA_KV_HEADS) % SWA_GROUPS_IN_FLIGHT == 0
    rope_offsets, rope_starts = _rope_tables(s, tile)
    return pl.pallas_call(
        _swa_kernel,
        grid=(b, s // tile),
        in_specs=[
            pl.BlockSpec(memory_space=pltpu.SMEM),
            pl.BlockSpec((None, tile, d), lambda bi, ti: (bi, ti, 0)),
            _w_all_spec("swa_q", d),
            _w_all_spec("swa_kv", d),
            _w_all_spec("swa_z", d),
            _w_all_spec("gates", d),
            pl.BlockSpec((tile, 2 * LANES), lambda bi, ti: (0, 0)),
            pl.BlockSpec((None, 1, 2 * LANES), lambda bi, ti: (ti, 0, 0)),
        ],
        out_specs=(pl.BlockSpec((None, tile, SWA_WIDTH), lambda bi, ti: (bi, ti, 0)),
                   pl.BlockSpec((None, tile, 2 * d), lambda bi, ti: (bi, ti, 0))),
        out_shape=(jax.ShapeDtypeStruct((b, s, SWA_WIDTH), BF16),
                   jax.ShapeDtypeStruct((b, s, 2 * d), BF16)),
        scratch_shapes=[
            pltpu.VMEM((2 * SWA_KV_HEADS, w + tile, LANES), BF16),
            pltpu.VMEM((2 * SWA_KV_HEADS, w + tile, LANES), BF16),
            pltpu.VMEM((tile, SWA_WIDTH), BF16),
            pltpu.VMEM((tile, SWA_WIDTH), F32),
        ],
        compiler_params=pltpu.CompilerParams(
            dimension_semantics=("arbitrary", "arbitrary"), vmem_limit_bytes=VMEM_LIMIT_BYTES),
        name="swa_branch",
    )(sinks, x, w_all, w_all, w_all, w_all, rope_offsets, rope_starts)


def _rope_tables(seq, tile):
    inv_freq = ROPE_THETA ** (-jnp.arange(0, ROPE_DIM, 2, dtype=F32) / ROPE_DIM)

    def lane_pattern(positions):
        ang = positions[:, None] * inv_freq[None, :]
        n = positions.shape[0]
        rest = SWA_HEAD_DIM - ROPE_DIM
        reps = LANES // SWA_HEAD_DIM
        cos = jnp.concatenate([jnp.cos(ang)] * 2 + [jnp.ones((n, rest), F32)], axis=1)
        sin = jnp.concatenate([jnp.sin(ang)] * 2 + [jnp.zeros((n, rest), F32)], axis=1)
        return jnp.concatenate([jnp.tile(cos, (1, reps)), jnp.tile(sin, (1, reps))], axis=1)

    offsets = lane_pattern(jnp.arange(tile, dtype=F32))
    starts = lane_pattern(jnp.arange(seq // tile, dtype=F32) * tile)
    return offsets, starts.reshape(seq // tile, 1, 2 * LANES)


def _merge_kernel(alpha, x_ref, ya_ref, yb_ref, g_ref, wa_ref, wb_ref, wo_ref, lng_ref, lnb_ref, o_ref,
                  merged_s):
    tile = x_ref.shape[0]
    sub = MERGE_SUB
    n_sub = tile // sub

    def gated_merge(j):
        rows = slice(j * sub, (j + 1) * sub)
        pa = jnp.dot(ya_ref[rows, :], wa_ref[...], preferred_element_type=F32)
        branch = g_ref[rows, :D_MODEL].astype(F32) * pa
        yield
        pb = jnp.dot(yb_ref[rows, :], wb_ref[...], preferred_element_type=F32)
        branch = branch + g_ref[rows, D_MODEL:].astype(F32) * pb
        yield
        merged_s[rows, :] = branch.astype(BF16)
        yield

    def project_norm(j):
        rows = slice(j * sub, (j + 1) * sub)
        out = jnp.dot(merged_s[rows, :], wo_ref[...], preferred_element_type=F32)
        yield
        r = alpha * x_ref[rows, :] + out
        mu = jnp.mean(r, axis=-1, keepdims=True)
        cen = r - mu
        var = jnp.mean(cen * cen, axis=-1, keepdims=True)
        o_ref[rows, :] = (cen * lax.rsqrt(var + LN_EPS) * lng_ref[...] + lnb_ref[...]).astype(o_ref.dtype)
        yield

    stages = ((gated_merge, 3), (project_norm, 2))
    for slot in range(n_sub + len(stages) - 1):
        _interleave([(stage(slot - k), steps) for k, (stage, steps) in enumerate(stages)
                     if 0 <= slot - k < n_sub])


def _merge(x2, ya2, yb2, gates2, wa, wb, wo, lng_row, lnb_row, alpha):
    n, d = x2.shape
    tile = MERGE_TILE
    assert d == D_MODEL and n % tile == 0 and tile % MERGE_SUB == 0
    const = lambda i: (0, 0)
    row = lambda i: (i, 0)
    return pl.pallas_call(
        functools.partial(_merge_kernel, alpha),
        grid=(n // tile,),
        in_specs=[
            pl.BlockSpec((tile, d), row),
            pl.BlockSpec((tile, ya2.shape[1]), row),
            pl.BlockSpec((tile, yb2.shape[1]), row),
            pl.BlockSpec((tile, gates2.shape[1]), row),
            pl.BlockSpec(wa.shape, const),
            pl.BlockSpec(wb.shape, const),
            pl.BlockSpec(wo.shape, const),
            pl.BlockSpec(lng_row.shape, const),
            pl.BlockSpec(lnb_row.shape, const),
        ],
        out_specs=pl.BlockSpec((tile, d), row),
        out_shape=jax.ShapeDtypeStruct((n, d), x2.dtype),
        scratch_shapes=[pltpu.VMEM((tile, d), BF16)],
        compiler_params=pltpu.CompilerParams(
            dimension_semantics=("arbitrary",), vmem_limit_bytes=VMEM_LIMIT_BYTES),
        name="merge_out_norm",
    )(x2, ya2, yb2, gates2, wa, wb, wo, lng_row, lnb_row)


def _lane_row(vec, offset):
    n = vec.shape[0]
    return jnp.pad(vec.astype(F32), (offset, LANES - offset - n)).reshape(1, LANES)


def _pack_w_in(w_in):
    o = _OFF_SWA
    groups = {
        "dn_qkv": w_in[:, _OFF_QKV:_OFF_DN_Z],
        "dn_z": w_in[:, _OFF_DN_Z:_OFF_DN_BA],
        "gates": w_in[:, _OFF_GATE:],
        "swa_q": w_in[:, o:o + SWA_WIDTH] * (SWA_HEAD_DIM ** -0.5),
        "swa_z": w_in[:, o + SWA_WIDTH + 2 * SWA_KV_WIDTH:_OFF_GATE],
        "swa_kv": w_in[:, o + SWA_WIDTH:o + SWA_WIDTH + 2 * SWA_KV_WIDTH],
        "dn_ba": jnp.pad(w_in[:, _OFF_DN_BA:_OFF_SWA], ((0, 0), (0, LANES - 2 * DN_HEADS))),
    }
    assert list(groups) == list(_W_ALL_LAYOUT)
    return jnp.concatenate([g.astype(BF16) for g in groups.values()], axis=1)


def _layer(x, w_in, conv_w, a_log, dt_bias, dn_norm_w, sinks, w_branch, w_out, ln_g, ln_b, alpha):
    b, s, d = x.shape
    w_all = _pack_w_in(w_in)
    y_a = _deltanet_branch(x, w_all, conv_w.astype(F32), _lane_row(a_log, DN_HEADS),
                           _lane_row(dt_bias, DN_HEADS), dn_norm_w.astype(F32).reshape(1, DN_HEAD_DIM))
    y_b, gates = _swa_branch(x, sinks.astype(F32), w_all)
    w_ab = w_branch.astype(BF16)
    out = _merge(x.reshape(b * s, d), y_a.reshape(b * s, DN_WIDTH), y_b.reshape(b * s, SWA_WIDTH),
                 gates.reshape(b * s, 2 * d), w_ab[0], w_ab[1], w_out.astype(BF16),
                 ln_g.astype(F32).reshape(1, d), ln_b.astype(F32).reshape(1, d), alpha)
    return out.reshape(b, s, d)


def kernel(x, w_in, conv_w, a_log, dt_bias, dn_norm_w, sinks, w_branch, w_out, ln_g, ln_b):
    depth = w_in.shape[0]
    assert w_in.shape[1:] == (D_MODEL, _OFF_GATE + 2 * D_MODEL) and conv_w.shape[1:] == (CONV_WIDTH, 3 * DN_WIDTH)
    alpha = (2.0 * depth) ** 0.25
    for layer in range(depth):
        x = _layer(x, w_in[layer], conv_w[layer], a_log[layer], dt_bias[layer], dn_norm_w[layer],
                   sinks[layer], w_branch[layer], w_out[layer], ln_g[layer], ln_b[layer], alpha)
    return x
```

```python
import functools

import jax
import jax.numpy as jnp
from jax import lax
from jax.experimental import pallas as pl
from jax.experimental.pallas import tpu as pltpu

F32 = jnp.float32
BF16 = jnp.bfloat16

D_MODEL = 1024
DN_HEADS = 4
DN_HEAD_DIM = 128
DN_WIDTH = DN_HEADS * DN_HEAD_DIM
CONV_WIDTH = 4
CHUNK = 64
SWA_Q_HEADS = 8
SWA_KV_HEADS = 2
SWA_HEAD_DIM = 64
SWA_WIDTH = SWA_Q_HEADS * SWA_HEAD_DIM
SWA_KV_WIDTH = SWA_KV_HEADS * SWA_HEAD_DIM
WINDOW = 128
ROPE_THETA = 500000.0
ROPE_DIM = SWA_HEAD_DIM // 4
LN_EPS = 1e-5
NORM_EPS = 1e-6
MASK_VALUE = -1e30

LANES = 128
SUBLANES = 8
VMEM_LIMIT_BYTES = 48 * 1024 * 1024

DN_TILE = 1024
DN_SUB = 256
DN_PROJ_COLS = 256
SWA_TILE = 1024
SWA_SUB = 256
SWA_GROUPS_IN_FLIGHT = 2
MERGE_TILE = 1024
MERGE_SUB = 256
MERGE_COLS = 256
assert CONV_WIDTH == 4

_OFF_QKV = 0
_OFF_DN_Z = 3 * DN_WIDTH
_OFF_DN_BA = _OFF_DN_Z + DN_WIDTH
_OFF_SWA = _OFF_DN_BA + 2 * DN_HEADS
_SWA_COLS = SWA_WIDTH + 2 * SWA_KV_WIDTH + SWA_WIDTH
_OFF_GATE = _OFF_SWA + _SWA_COLS


def _make_w_all_layout():
    widths = (("dn_qkv", 3 * DN_WIDTH), ("dn_z", DN_WIDTH), ("gates", 2 * D_MODEL), ("swa_q", SWA_WIDTH),
              ("swa_z", SWA_WIDTH), ("swa_kv", 2 * SWA_KV_WIDTH), ("dn_ba", LANES))
    layout, off = {}, 0
    for name, width in widths:
        assert off % width == 0
        layout[name] = (off, width)
        off += width
    return layout


_W_ALL_LAYOUT = _make_w_all_layout()


def _mm(a, b):
    return jnp.dot(a.astype(BF16), b.astype(BF16), preferred_element_type=F32)


def _mm_nt(a, b):
    return lax.dot_general(a.astype(BF16), b.astype(BF16), (((1,), (1,)), ((), ())),
                           preferred_element_type=F32)


def _mm_tn(a, b):
    return lax.dot_general(a.astype(BF16), b.astype(BF16), (((0,), (0,)), ((), ())),
                           preferred_element_type=F32)


def _silu(v):
    return v * jax.nn.sigmoid(v)


def _interleave(streams):
    live = [[gen, 0, steps] for gen, steps in streams]
    while live:
        entry = min(live, key=lambda e: e[1] / e[2])
        try:
            next(entry[0])
            entry[1] += 1
        except StopIteration:
            live.remove(entry)


def _dn_kernel(x_ref, wqkv_ref, wz_ref, wba_ref, convw_ref, alog_ref, dtb_ref, normw_ref, y_ref,
               xb_s, hbuf, q_s, k_s, v_s, z_s, beta_s, gc_s, gct_s, u_s, w_s, qd_s, kt_s, a_s, state):
    t = pl.program_id(1)
    tile = x_ref.shape[0]
    sub = DN_SUB
    n_sub = tile // sub
    halo = SUBLANES
    pair = 2 * CHUNK
    pairs_per_sub = sub // pair
    heads = range(DN_HEADS)

    @pl.when(t == 0)
    def _():
        hbuf[0:halo, :] = jnp.zeros((halo, 3 * DN_WIDTH), F32)
        state[...] = jnp.zeros_like(state)

    row = lax.broadcasted_iota(jnp.int32, (CHUNK, pair), 0)
    lane = lax.broadcasted_iota(jnp.int32, (CHUNK, pair), 1)
    col = lane % CHUNK
    left = lane < CHUNK
    causal = row >= col
    strict = row > col
    xor_ij = row ^ col
    eye = jnp.where(row == col, 1.0, 0.0).astype(F32)
    first_chunk = lax.broadcasted_iota(jnp.int32, (pair, LANES), 0) < CHUNK
    pos = lax.broadcasted_iota(jnp.int32, (sub, LANES), 0) % CHUNK

    def head_cols(h):
        return slice(h * DN_HEAD_DIM, (h + 1) * DN_HEAD_DIM)

    def side_by_side(m):
        return jnp.where(left, m[:CHUNK], m[CHUNK:])

    def block_diag(m):
        zero = jnp.zeros_like(m)
        return jnp.concatenate([jnp.where(left, m, zero), jnp.where(left, zero, m)], axis=0)

    def front(j):
        r0 = j * sub
        rows = slice(r0, r0 + sub)
        xb_s[rows, :] = x_ref[rows, :].astype(BF16)
        ba = jnp.dot(xb_s[rows, :], wba_ref[...], preferred_element_type=F32)
        yield
        beta_s[rows, :] = jax.nn.sigmoid(ba)
        xg = ba + dtb_ref[...]
        softplus = jnp.maximum(xg, 0.0) + jnp.log1p(jnp.exp(-jnp.abs(xg)))
        gc = -jnp.exp(alog_ref[...]) * softplus
        step = 1
        while step < CHUNK:
            gc = gc + jnp.where(pos >= step, pltpu.roll(gc, step, 0), 0.0)
            step *= 2
        gc_s[rows, :] = gc
        gct_s[:, rows] = gc.T
        yield
        dests = (q_s, k_s, v_s)
        slabs_per_chunk = DN_PROJ_COLS // DN_HEAD_DIM
        for s in range(3 * DN_HEADS):
            if s % slabs_per_chunk == 0:
                chunk = slice(s * DN_HEAD_DIM, s * DN_HEAD_DIM + DN_PROJ_COLS)
                hbuf[halo + r0:halo + r0 + sub, chunk] = jnp.dot(
                    xb_s[rows, :], wqkv_ref[:, chunk], preferred_element_type=F32)
                z0 = (s // slabs_per_chunk) * DN_PROJ_COLS
                if z0 < DN_WIDTH:
                    z_s[rows, z0:z0 + DN_PROJ_COLS] = jnp.dot(
                        xb_s[rows, :], wz_ref[:, z0:z0 + DN_PROJ_COLS], preferred_element_type=F32)
                yield
            cols = slice(s * DN_HEAD_DIM, (s + 1) * DN_HEAD_DIM)
            ext = hbuf[r0:r0 + halo + sub, cols]
            prev = pltpu.roll(ext, 1, 0)
            older = convw_ref[0:1, cols] * prev + convw_ref[1:2, cols] * ext
            newer = convw_ref[2:3, cols] * prev + convw_ref[3:4, cols] * ext
            acc = (pltpu.roll(older, 2, 0) + newer)[halo:, :]
            yield
            a = _silu(acc)
            if s < 2 * DN_HEADS:
                a = a * lax.rsqrt(jnp.sum(a * a, axis=-1, keepdims=True) + NORM_EPS)
                if s < DN_HEADS:
                    a = a * (DN_HEAD_DIM ** -0.5)
            dests[s // DN_HEADS][rows, head_cols(s % DN_HEADS)] = a
            yield
        if j == n_sub - 1:
            hbuf[0:halo, :] = hbuf[tile:tile + halo, :]

    front_steps = 2 + 3 * DN_WIDTH // DN_PROJ_COLS + 2 * 3 * DN_HEADS

    def factor(j):
        problems = [(j * sub + p * pair, h) for p in range(pairs_per_sub) for h in heads]
        lows, rhs = [], []
        for r0, h in problems:
            rows = slice(r0, r0 + pair)
            la = DN_HEADS + h
            if h == 0:
                gc_p = gc_s[rows, :]
                e_gc = jnp.exp(gc_p)
                g_end = jnp.where(first_chunk, gc_p[CHUNK - 1:CHUNK, :], gc_p[pair - 1:pair, :])
                e_tail = jnp.exp(g_end - gc_p)
                beta_p = beta_s[rows, :]
            g_col = gc_p[:, la:la + 1]
            g_row = gct_s[la:la + 1, rows]
            b_col = beta_p[:, h:h + 1]
            eg_col = e_gc[:, la:la + 1]
            qh = q_s[rows, head_cols(h)]
            kh = k_s[rows, head_cols(h)]
            vh = v_s[rows, head_cols(h)]
            g_diff = jnp.where(left, g_col[:CHUNK], g_col[CHUNK:]) - g_row
            decay = jnp.where(causal, jnp.exp(jnp.where(causal, g_diff, 0.0)), 0.0)
            kb = kh * b_col
            kq = _mm_nt(jnp.concatenate([kb, qh], axis=0), kh)
            lows.append(jnp.where(strict, side_by_side(kq[:pair]) * decay, 0.0))
            a_s[rows, head_cols(h)] = block_diag((side_by_side(kq[pair:]) * decay).astype(BF16))
            rhs.append(jnp.concatenate([vh * b_col, kb * eg_col], axis=1).astype(BF16))
            qd_s[rows, head_cols(h)] = (qh * eg_col).astype(BF16)
            kt_s[rows, head_cols(h)] = (kh * e_tail[:, la:la + 1]).astype(BF16)
            yield
        invs = [eye - jnp.where(xor_ij == 1, low, 0.0) for low in lows]
        level = 1
        while (1 << level) < CHUNK:
            joins = (xor_ij >> level) == 1
            cs = [block_diag(jnp.where(joins, low, 0.0).astype(BF16)) for low in lows]
            xs = [inv.astype(BF16) for inv in invs]
            xcs = []
            for x, c in zip(xs, cs):
                xcs.append(_mm(x, c))
                yield
            for i, (xc, x) in enumerate(zip(xcs, xs)):
                invs[i] = invs[i] - _mm(xc, block_diag(x))
                yield
            level += 1
        for (r0, h), t_inv, r in zip(problems, invs, rhs):
            rows = slice(r0, r0 + pair)
            uw = _mm(block_diag(t_inv.astype(BF16)), r)
            u_s[rows, head_cols(h)] = uw[:, :DN_HEAD_DIM]
            w_s[rows, head_cols(h)] = uw[:, DN_HEAD_DIM:].astype(BF16)
            yield

    levels = CHUNK.bit_length() - 2
    factor_steps = (2 + 2 * levels) * pairs_per_sub * DN_HEADS

    def recur(j):
        for p in range(pairs_per_sub):
            p0 = j * sub + p * pair
            v_new = [[None, None] for _ in heads]
            q_state = [[None, None] for _ in heads]
            for cc in range(2):
                r0 = p0 + cc * CHUNK
                rows = slice(r0, r0 + CHUNK)
                e_end = jnp.exp(gc_s[r0 + CHUNK - 1:r0 + CHUNK, :])
                s_in = [state[h] for h in heads]
                prods = []
                for h in heads:
                    lhs = jnp.concatenate([w_s[rows, head_cols(h)], qd_s[rows, head_cols(h)]], axis=0)
                    prods.append(_mm(lhs, s_in[h]))
                for h in heads:
                    v_new[h][cc] = u_s[rows, head_cols(h)] - prods[h][:CHUNK]
                    q_state[h][cc] = prods[h][CHUNK:]
                yield
                for h in heads:
                    la = DN_HEADS + h
                    state[h] = (s_in[h] * e_end[:, la:la + 1]
                                + _mm_tn(kt_s[rows, head_cols(h)], v_new[h][cc]))
                yield
            rows = slice(p0, p0 + pair)
            for h in heads:
                o = (jnp.concatenate(q_state[h], axis=0)
                     + _mm(a_s[rows, head_cols(h)], jnp.concatenate(v_new[h], axis=0)))
                o = o * lax.rsqrt(jnp.mean(o * o, axis=-1, keepdims=True) + NORM_EPS) * normw_ref[...]
                y_ref[rows, head_cols(h)] = (o * _silu(z_s[rows, head_cols(h)])).astype(y_ref.dtype)
                yield

    recur_steps = (4 + DN_HEADS) * pairs_per_sub

    stages = ((front, front_steps), (factor, factor_steps), (recur, recur_steps))
    for slot in range(n_sub + len(stages) - 1):
        _interleave([(stage(slot - k), steps) for k, (stage, steps) in enumerate(stages)
                     if 0 <= slot - k < n_sub])


def _w_all_spec(name, rows):
    off, width = _W_ALL_LAYOUT[name]
    return pl.BlockSpec((rows, width), lambda *_: (0, off // width))


def _deltanet_branch(x, w_all, conv_w, alog_row, dtb_row, normw_row):
    b, s, d = x.shape
    tile = DN_TILE
    assert d == D_MODEL and s % tile == 0 and tile % DN_SUB == 0 and DN_SUB % (2 * CHUNK) == 0
    assert (3 * DN_WIDTH) % DN_PROJ_COLS == 0 and DN_PROJ_COLS % DN_HEAD_DIM == 0
    const = lambda bi, ti: (0, 0)
    return pl.pallas_call(
        _dn_kernel,
        grid=(b, s // tile),
        in_specs=[
            pl.BlockSpec((None, tile, d), lambda bi, ti: (bi, ti, 0)),
            _w_all_spec("dn_qkv", d),
            _w_all_spec("dn_z", d),
            _w_all_spec("dn_ba", d),
            pl.BlockSpec(conv_w.shape, const),
            pl.BlockSpec(alog_row.shape, const),
            pl.BlockSpec(dtb_row.shape, const),
            pl.BlockSpec(normw_row.shape, const),
        ],
        out_specs=pl.BlockSpec((None, tile, DN_WIDTH), lambda bi, ti: (bi, ti, 0)),
        out_shape=jax.ShapeDtypeStruct((b, s, DN_WIDTH), BF16),
        scratch_shapes=[
            pltpu.VMEM((tile, d), BF16),
            pltpu.VMEM((tile + SUBLANES, 3 * DN_WIDTH), F32),
            pltpu.VMEM((tile, DN_WIDTH), F32),
            pltpu.VMEM((tile, DN_WIDTH), F32),
            pltpu.VMEM((tile, DN_WIDTH), F32),
            pltpu.VMEM((tile, DN_WIDTH), F32),
            pltpu.VMEM((tile, LANES), F32),
            pltpu.VMEM((tile, LANES), F32),
            pltpu.VMEM((LANES, tile), F32),
            pltpu.VMEM((tile, DN_WIDTH), F32),
            pltpu.VMEM((tile, DN_WIDTH), BF16),
            pltpu.VMEM((tile, DN_WIDTH), BF16),
            pltpu.VMEM((tile, DN_WIDTH), BF16),
            pltpu.VMEM((tile, DN_WIDTH), BF16),
            pltpu.VMEM((DN_HEADS, DN_HEAD_DIM, DN_HEAD_DIM), F32),
        ],
        compiler_params=pltpu.CompilerParams(
            dimension_semantics=("arbitrary", "arbitrary"), vmem_limit_bytes=VMEM_LIMIT_BYTES),
        name="deltanet_branch",
    )(x, w_all, w_all, w_all, conv_w, alog_row, dtb_row, normw_row)


def _swa_kernel(sinks_ref, x_ref, wq_ref, wkv_ref, wz_ref, rope_off_ref, rope_start_ref, y_ref,
                kband, vband, q_s, z_s):
    t = pl.program_id(1)
    w = WINDOW
    tile = x_ref.shape[0]

    @pl.when(t == 0)
    def _():
        kband[:, 0:w, :] = jnp.zeros((2 * SWA_KV_HEADS, w, LANES), BF16)
        vband[:, 0:w, :] = jnp.zeros((2 * SWA_KV_HEADS, w, LANES), BF16)

    half = ROPE_DIM // 2
    sub = SWA_SUB
    n_sub = tile // sub
    n_pairs = SWA_Q_HEADS // 2
    group = SWA_Q_HEADS // SWA_KV_HEADS
    lane = lax.broadcasted_iota(jnp.int32, (sub, LANES), 1)
    lo = lane < SWA_HEAD_DIM
    head_lane = lane % SWA_HEAD_DIM

    qi = lax.broadcasted_iota(jnp.int32, (w, 2 * w), 0)
    kj = lax.broadcasted_iota(jnp.int32, (w, 2 * w), 1)
    in_band = (kj > qi) & (kj <= qi + w)
    bias = jnp.where(in_band, 0.0, MASK_VALUE).astype(F32)
    bias_first = jnp.where(in_band & (kj >= jnp.where(t == 0, w, 0)), 0.0, MASK_VALUE).astype(F32)

    def front(j):
        r0 = j * sub
        rows = slice(r0, r0 + sub)
        cos_o = rope_off_ref[rows, 0:LANES]
        sin_o = rope_off_ref[rows, LANES:2 * LANES]
        cos_s = rope_start_ref[:, 0:LANES]
        sin_s = rope_start_ref[:, LANES:2 * LANES]
        cos_p = cos_s * cos_o - sin_s * sin_o
        sin_p = sin_s * cos_o + cos_s * sin_o
        sin_a = jnp.where(head_lane < half, -sin_p, 0.0)
        sin_b = jnp.where(head_lane >= half, sin_p, 0.0)

        def rope(v):
            return v * cos_p + pltpu.roll(v, LANES - half, 1) * sin_a + pltpu.roll(v, half, 1) * sin_b

        xb = x_ref[rows, :].astype(BF16)
        q = jnp.dot(xb, wq_ref[...], preferred_element_type=F32)
        yield
        for pair in range(n_pairs):
            p0 = pair * LANES
            q_s[rows, p0:p0 + LANES] = rope(q[:, p0:p0 + LANES]).astype(BF16)
        yield
        kv = jnp.dot(xb, wkv_ref[...], preferred_element_type=F32)
        z_s[rows, :] = jnp.dot(xb, wz_ref[...], preferred_element_type=F32)
        yield
        k = rope(kv[:, :LANES])
        v = kv[:, LANES:]
        band_rows = slice(w + r0, w + r0 + sub)
        for src, band in ((k, kband), (v, vband)):
            swapped = pltpu.roll(src, SWA_HEAD_DIM, 1)
            band[0, band_rows, :] = jnp.where(lo, src, 0.0).astype(BF16)
            band[1, band_rows, :] = jnp.where(lo, 0.0, swapped).astype(BF16)
            band[2, band_rows, :] = jnp.where(lo, swapped, 0.0).astype(BF16)
            band[3, band_rows, :] = jnp.where(lo, 0.0, src).astype(BF16)
        yield

    front_steps = 4

    slabs_per_group = group // 2
    upper_rows = lax.broadcasted_iota(jnp.int32, (slabs_per_group * w, 1), 0) >= w

    def kv_group(r0, g, blk_bias):
        slabs = [slice((slabs_per_group * g + i) * LANES, (slabs_per_group * g + i + 1) * LANES)
                 for i in range(slabs_per_group)]
        q_rows = jnp.concatenate([q_s[r0:r0 + w, cols] for cols in slabs], axis=0)
        bias_rows = jnp.concatenate([blk_bias] * slabs_per_group, axis=0)
        scores, sinks = [], []
        for hf in range(2):
            heads_hf = [group * g + 2 * i + hf for i in range(slabs_per_group)]
            sinks.append(jnp.where(upper_rows, sinks_ref[heads_hf[1]], sinks_ref[heads_hf[0]]))
            scores.append(_mm_nt(q_rows, kband[2 * g + hf, r0:r0 + 2 * w, :]) + bias_rows)
        yield
        probs, denoms = [], []
        for s, sink in zip(scores, sinks):
            m = jnp.maximum(jnp.max(jnp.maximum(s[:, :w], s[:, w:]), axis=-1, keepdims=True), sink)
            p = jnp.exp(s - m)
            denoms.append(jnp.sum(p[:, :w] + p[:, w:], axis=-1, keepdims=True) + jnp.exp(sink - m))
            probs.append(p.astype(BF16))
        yield
        acc = None
        for hf in range(2):
            o = _mm(probs[hf], vband[2 * g + hf, r0:r0 + 2 * w, :]) / denoms[hf]
            acc = o if acc is None else acc + o
        for i, cols in enumerate(slabs):
            y_ref[r0:r0 + w, cols] = (
                acc[i * w:(i + 1) * w] * _silu(z_s[r0:r0 + w, cols])).astype(y_ref.dtype)
        yield

    def attend(j):
        items = [(j * sub + b * w, g) for b in range(sub // w) for g in range(SWA_KV_HEADS)]
        for g0 in range(0, len(items), SWA_GROUPS_IN_FLIGHT):
            streams = [kv_group(r0, g, bias_first if r0 == 0 else bias)
                       for r0, g in items[g0:g0 + SWA_GROUPS_IN_FLIGHT]]
            for _ in range(3):
                for stream in streams:
                    next(stream)
                yield

    attend_steps = 3 * (sub // w) * SWA_KV_HEADS // SWA_GROUPS_IN_FLIGHT

    stages = ((front, front_steps), (attend, attend_steps))
    for slot in range(n_sub + len(stages) - 1):
        _interleave([(stage(slot - k), steps) for k, (stage, steps) in enumerate(stages)
                     if 0 <= slot - k < n_sub])

    kband[:, 0:w, :] = kband[:, tile:tile + w, :]
    vband[:, 0:w, :] = vband[:, tile:tile + w, :]


def _swa_branch(x, sinks, w_all):
    b, s, d = x.shape
    w = WINDOW
    tile = SWA_TILE
    assert d == D_MODEL and s % tile == 0 and tile % SWA_SUB == 0 and SWA_SUB % w == 0
    assert SWA_Q_HEADS // SWA_KV_HEADS == 4 and 2 * SWA_HEAD_DIM == LANES
    assert (SWA_SUB // w * SWA_KV_HEADS) % SWA_GROUPS_IN_FLIGHT == 0
    rope_offsets, rope_starts = _rope_tables(s, tile)
    return pl.pallas_call(
        _swa_kernel,
        grid=(b, s // tile),
        in_specs=[
            pl.BlockSpec(memory_space=pltpu.SMEM),
            pl.BlockSpec((None, tile, d), lambda bi, ti: (bi, ti, 0)),
            _w_all_spec("swa_q", d),
            _w_all_spec("swa_kv", d),
            _w_all_spec("swa_z", d),
            pl.BlockSpec((tile, 2 * LANES), lambda bi, ti: (0, 0)),
            pl.BlockSpec((None, 1, 2 * LANES), lambda bi, ti: (ti, 0, 0)),
        ],
        out_specs=pl.BlockSpec((None, tile, SWA_WIDTH), lambda bi, ti: (bi, ti, 0)),
        out_shape=jax.ShapeDtypeStruct((b, s, SWA_WIDTH), BF16),
        scratch_shapes=[
            pltpu.VMEM((2 * SWA_KV_HEADS, w + tile, LANES), BF16),
            pltpu.VMEM((2 * SWA_KV_HEADS, w + tile, LANES), BF16),
            pltpu.VMEM((tile, SWA_WIDTH), BF16),
            pltpu.VMEM((tile, SWA_WIDTH), F32),
        ],
        compiler_params=pltpu.CompilerParams(
            dimension_semantics=("arbitrary", "arbitrary"), vmem_limit_bytes=VMEM_LIMIT_BYTES),
        name="swa_branch",
    )(sinks, x, w_all, w_all, w_all, rope_offsets, rope_starts)


def _rope_tables(seq, tile):
    inv_freq = ROPE_THETA ** (-jnp.arange(0, ROPE_DIM, 2, dtype=F32) / ROPE_DIM)

    def lane_pattern(positions):
        ang = positions[:, None] * inv_freq[None, :]
        n = positions.shape[0]
        rest = SWA_HEAD_DIM - ROPE_DIM
        reps = LANES // SWA_HEAD_DIM
        cos = jnp.concatenate([jnp.cos(ang)] * 2 + [jnp.ones((n, rest), F32)], axis=1)
        sin = jnp.concatenate([jnp.sin(ang)] * 2 + [jnp.zeros((n, rest), F32)], axis=1)
        return jnp.concatenate([jnp.tile(cos, (1, reps)), jnp.tile(sin, (1, reps))], axis=1)

    offsets = lane_pattern(jnp.arange(tile, dtype=F32))
    starts = lane_pattern(jnp.arange(seq // tile, dtype=F32) * tile)
    return offsets, starts.reshape(seq // tile, 1, 2 * LANES)


def _merge_kernel(alpha, x_ref, ya_ref, yb_ref, wg_ref, wa_ref, wb_ref, wo_ref, lng_ref, lnb_ref, o_ref,
                  merged_s):
    tile = x_ref.shape[0]
    sub = MERGE_SUB
    n_sub = tile // sub

    def gated_merge(j):
        rows = slice(j * sub, (j + 1) * sub)
        xb = x_ref[rows, :].astype(BF16)
        ya = ya_ref[rows, :]
        yb = yb_ref[rows, :]
        for c0 in range(0, D_MODEL, MERGE_COLS):
            cols = slice(c0, c0 + MERGE_COLS)
            gate_cols = slice(D_MODEL + c0, D_MODEL + c0 + MERGE_COLS)
            gate_a = jax.nn.sigmoid(jnp.dot(xb, wg_ref[:, cols], preferred_element_type=F32))
            branch = gate_a * jnp.dot(ya, wa_ref[:, cols], preferred_element_type=F32)
            gate_b = jax.nn.sigmoid(jnp.dot(xb, wg_ref[:, gate_cols], preferred_element_type=F32))
            branch = branch + gate_b * jnp.dot(yb, wb_ref[:, cols], preferred_element_type=F32)
            merged_s[rows, cols] = branch.astype(BF16)
            yield

    def project_norm(j):
        rows = slice(j * sub, (j + 1) * sub)
        out = jnp.dot(merged_s[rows, :], wo_ref[...], preferred_element_type=F32)
        yield
        r = alpha * x_ref[rows, :] + out
        mu = jnp.mean(r, axis=-1, keepdims=True)
        cen = r - mu
        var = jnp.mean(cen * cen, axis=-1, keepdims=True)
        o_ref[rows, :] = (cen * lax.rsqrt(var + LN_EPS) * lng_ref[...] + lnb_ref[...]).astype(o_ref.dtype)
        yield

    stages = ((gated_merge, D_MODEL // MERGE_COLS), (project_norm, 2))
    for slot in range(n_sub + len(stages) - 1):
        _interleave([(stage(slot - k), steps) for k, (stage, steps) in enumerate(stages)
                     if 0 <= slot - k < n_sub])


def _merge(x2, ya2, yb2, w_all, wa, wb, wo, lng_row, lnb_row, alpha):
    n, d = x2.shape
    tile = MERGE_TILE
    assert d == D_MODEL and n % tile == 0 and tile % MERGE_SUB == 0
    const = lambda i: (0, 0)
    row = lambda i: (i, 0)
    return pl.pallas_call(
        functools.partial(_merge_kernel, alpha),
        grid=(n // tile,),
        in_specs=[
            pl.BlockSpec((tile, d), row),
            pl.BlockSpec((tile, ya2.shape[1]), row),
            pl.BlockSpec((tile, yb2.shape[1]), row),
            _w_all_spec("gates", d),
            pl.BlockSpec(wa.shape, const),
            pl.BlockSpec(wb.shape, const),
            pl.BlockSpec(wo.shape, const),
            pl.BlockSpec(lng_row.shape, const),
            pl.BlockSpec(lnb_row.shape, const),
        ],
        out_specs=pl.BlockSpec((tile, d), row),
        out_shape=jax.ShapeDtypeStruct((n, d), x2.dtype),
        scratch_shapes=[pltpu.VMEM((tile, d), BF16)],
        compiler_params=pltpu.CompilerParams(
            dimension_semantics=("arbitrary",), vmem_limit_bytes=VMEM_LIMIT_BYTES),
        name="merge_out_norm",
    )(x2, ya2, yb2, w_all, wa, wb, wo, lng_row, lnb_row)


def _lane_row(vec, offset):
    n = vec.shape[0]
    return jnp.pad(vec.astype(F32), (offset, LANES - offset - n)).reshape(1, LANES)


def _pack_w_in(w_in):
    o = _OFF_SWA
    groups = {
        "dn_qkv": w_in[:, _OFF_QKV:_OFF_DN_Z],
        "dn_z": w_in[:, _OFF_DN_Z:_OFF_DN_BA],
        "gates": w_in[:, _OFF_GATE:],
        "swa_q": w_in[:, o:o + SWA_WIDTH] * (SWA_HEAD_DIM ** -0.5),
        "swa_z": w_in[:, o + SWA_WIDTH + 2 * SWA_KV_WIDTH:_OFF_GATE],
        "swa_kv": w_in[:, o + SWA_WIDTH:o + SWA_WIDTH + 2 * SWA_KV_WIDTH],
        "dn_ba": jnp.pad(w_in[:, _OFF_DN_BA:_OFF_SWA], ((0, 0), (0, LANES - 2 * DN_HEADS))),
    }
    assert list(groups) == list(_W_ALL_LAYOUT)
    return jnp.concatenate([g.astype(BF16) for g in groups.values()], axis=1)


def _layer(x, w_in, conv_w, a_log, dt_bias, dn_norm_w, sinks, w_branch, w_out, ln_g, ln_b, alpha):
    b, s, d = x.shape
    w_all = _pack_w_in(w_in)
    y_a = _deltanet_branch(x, w_all, conv_w.astype(F32), _lane_row(a_log, DN_HEADS),
                           _lane_row(dt_bias, DN_HEADS), dn_norm_w.astype(F32).reshape(1, DN_HEAD_DIM))
    y_b = _swa_branch(x, sinks.astype(F32), w_all)
    w_ab = w_branch.astype(BF16)
    out = _merge(x.reshape(b * s, d), y_a.reshape(b * s, DN_WIDTH), y_b.reshape(b * s, SWA_WIDTH),
                 w_all, w_ab[0], w_ab[1], w_out.astype(BF16),
                 ln_g.astype(F32).reshape(1, d), ln_b.astype(F32).reshape(1, d), alpha)
    return out.reshape(b, s, d)


def kernel(x, w_in, conv_w, a_log, dt_bias, dn_norm_w, sinks, w_branch, w_out, ln_g, ln_b):
    depth = w_in.shape[0]
    assert w_in.shape[1:] == (D_MODEL, _OFF_GATE + 2 * D_MODEL) and conv_w.shape[1:] == (CONV_WIDTH, 3 * DN_WIDTH)
    alpha = (2.0 * depth) ** 0.25
    for layer in range(depth):
        x = _layer(x, w_in[layer], conv_w[layer], a_log[layer], dt_bias[layer], dn_norm_w[layer],
                   sinks[layer], w_branch[layer], w_out[layer], ln_g[layer], ln_b[layer], alpha)
    return x
```

```python
import functools

import jax
import jax.numpy as jnp
from jax import lax
from jax.experimental import pallas as pl
from jax.experimental.pallas import tpu as pltpu

F32 = jnp.float32
BF16 = jnp.bfloat16

D_MODEL = 1024
DN_HEADS = 4
DN_HEAD_DIM = 128
DN_WIDTH = DN_HEADS * DN_HEAD_DIM
CONV_WIDTH = 4
CHUNK = 64
SWA_Q_HEADS = 8
SWA_KV_HEADS = 2
SWA_HEAD_DIM = 64
SWA_WIDTH = SWA_Q_HEADS * SWA_HEAD_DIM
SWA_KV_WIDTH = SWA_KV_HEADS * SWA_HEAD_DIM
WINDOW = 128
ROPE_THETA = 500000.0
ROPE_DIM = SWA_HEAD_DIM // 4
LN_EPS = 1e-5
NORM_EPS = 1e-6
MASK_VALUE = -1e30

LANES = 128
SUBLANES = 8
VMEM_LIMIT_BYTES = 48 * 1024 * 1024

DN_TILE = 1024
DN_SUB = 256
DN_PROJ_COLS = 256
SWA_TILE = 1024
SWA_SUB = 256
SWA_GROUPS_IN_FLIGHT = 2
MERGE_TILE = 1024
MERGE_SUB = 256
assert CONV_WIDTH == 4

_OFF_QKV = 0
_OFF_DN_Z = 3 * DN_WIDTH
_OFF_DN_BA = _OFF_DN_Z + DN_WIDTH
_OFF_SWA = _OFF_DN_BA + 2 * DN_HEADS
_SWA_COLS = SWA_WIDTH + 2 * SWA_KV_WIDTH + SWA_WIDTH
_OFF_GATE = _OFF_SWA + _SWA_COLS


def _make_w_all_layout():
    widths = (("dn_qkv", 3 * DN_WIDTH), ("dn_z", DN_WIDTH), ("gates", 2 * D_MODEL), ("swa_q", SWA_WIDTH),
              ("swa_z", SWA_WIDTH), ("swa_kv", 2 * SWA_KV_WIDTH), ("dn_ba", LANES))
    layout, off = {}, 0
    for name, width in widths:
        assert off % width == 0
        layout[name] = (off, width)
        off += width
    return layout


_W_ALL_LAYOUT = _make_w_all_layout()


def _mm(a, b):
    return jnp.dot(a.astype(BF16), b.astype(BF16), preferred_element_type=F32)


def _mm_nt(a, b):
    return lax.dot_general(a.astype(BF16), b.astype(BF16), (((1,), (1,)), ((), ())),
                           preferred_element_type=F32)


def _mm_tn(a, b):
    return lax.dot_general(a.astype(BF16), b.astype(BF16), (((0,), (0,)), ((), ())),
                           preferred_element_type=F32)


def _silu(v):
    return v * jax.nn.sigmoid(v)


def _interleave(streams):
    live = [[gen, 0, steps] for gen, steps in streams]
    while live:
        entry = min(live, key=lambda e: e[1] / e[2])
        try:
            next(entry[0])
            entry[1] += 1
        except StopIteration:
            live.remove(entry)


def _dn_kernel(x_ref, wqkv_ref, wz_ref, wba_ref, convw_ref, alog_ref, dtb_ref, normw_ref, y_ref,
               xb_s, hbuf, q_s, k_s, v_s, z_s, beta_s, gc_s, gct_s, u_s, w_s, qd_s, kt_s, a_s, state):
    t = pl.program_id(1)
    tile = x_ref.shape[0]
    sub = DN_SUB
    n_sub = tile // sub
    halo = SUBLANES
    pair = 2 * CHUNK
    pairs_per_sub = sub // pair
    heads = range(DN_HEADS)

    @pl.when(t == 0)
    def _():
        hbuf[0:halo, :] = jnp.zeros((halo, 3 * DN_WIDTH), F32)
        state[...] = jnp.zeros_like(state)

    row = lax.broadcasted_iota(jnp.int32, (CHUNK, pair), 0)
    lane = lax.broadcasted_iota(jnp.int32, (CHUNK, pair), 1)
    col = lane % CHUNK
    left = lane < CHUNK
    causal = row >= col
    strict = row > col
    xor_ij = row ^ col
    eye = jnp.where(row == col, 1.0, 0.0).astype(F32)
    first_chunk = lax.broadcasted_iota(jnp.int32, (pair, LANES), 0) < CHUNK
    pos = lax.broadcasted_iota(jnp.int32, (sub, LANES), 0) % CHUNK

    def head_cols(h):
        return slice(h * DN_HEAD_DIM, (h + 1) * DN_HEAD_DIM)

    def side_by_side(m):
        return jnp.where(left, m[:CHUNK], m[CHUNK:])

    def block_diag(m):
        zero = jnp.zeros_like(m)
        return jnp.concatenate([jnp.where(left, m, zero), jnp.where(left, zero, m)], axis=0)

    def front(j):
        r0 = j * sub
        rows = slice(r0, r0 + sub)
        xb_s[rows, :] = x_ref[rows, :].astype(BF16)
        ba = jnp.dot(xb_s[rows, :], wba_ref[...], preferred_element_type=F32)
        yield
        beta_s[rows, :] = jax.nn.sigmoid(ba)
        xg = ba + dtb_ref[...]
        softplus = jnp.maximum(xg, 0.0) + jnp.log1p(jnp.exp(-jnp.abs(xg)))
        gc = -jnp.exp(alog_ref[...]) * softplus
        step = 1
        while step < CHUNK:
            gc = gc + jnp.where(pos >= step, pltpu.roll(gc, step, 0), 0.0)
            step *= 2
        gc_s[rows, :] = gc
        gct_s[:, rows] = gc.T
        yield
        dests = (q_s, k_s, v_s)
        slabs_per_chunk = DN_PROJ_COLS // DN_HEAD_DIM
        for s in range(3 * DN_HEADS):
            if s % slabs_per_chunk == 0:
                chunk = slice(s * DN_HEAD_DIM, s * DN_HEAD_DIM + DN_PROJ_COLS)
                hbuf[halo + r0:halo + r0 + sub, chunk] = jnp.dot(
                    xb_s[rows, :], wqkv_ref[:, chunk], preferred_element_type=F32)
                z0 = (s // slabs_per_chunk) * DN_PROJ_COLS
                if z0 < DN_WIDTH:
                    z_s[rows, z0:z0 + DN_PROJ_COLS] = jnp.dot(
                        xb_s[rows, :], wz_ref[:, z0:z0 + DN_PROJ_COLS], preferred_element_type=F32)
                yield
            cols = slice(s * DN_HEAD_DIM, (s + 1) * DN_HEAD_DIM)
            ext = hbuf[r0:r0 + halo + sub, cols]
            prev = pltpu.roll(ext, 1, 0)
            older = convw_ref[0:1, cols] * prev + convw_ref[1:2, cols] * ext
            newer = convw_ref[2:3, cols] * prev + convw_ref[3:4, cols] * ext
            acc = (pltpu.roll(older, 2, 0) + newer)[halo:, :]
            yield
            a = _silu(acc)
            if s < 2 * DN_HEADS:
                a = a * lax.rsqrt(jnp.sum(a * a, axis=-1, keepdims=True) + NORM_EPS)
                if s < DN_HEADS:
                    a = a * (DN_HEAD_DIM ** -0.5)
            dests[s // DN_HEADS][rows, head_cols(s % DN_HEADS)] = a
            yield
        if j == n_sub - 1:
            hbuf[0:halo, :] = hbuf[tile:tile + halo, :]

    front_steps = 2 + 3 * DN_WIDTH // DN_PROJ_COLS + 2 * 3 * DN_HEADS

    def factor(j):
        problems = [(j * sub + p * pair, h) for p in range(pairs_per_sub) for h in heads]
        lows, rhs = [], []
        for r0, h in problems:
            rows = slice(r0, r0 + pair)
            la = DN_HEADS + h
            if h == 0:
                gc_p = gc_s[rows, :]
                e_gc = jnp.exp(gc_p)
                g_end = jnp.where(first_chunk, gc_p[CHUNK - 1:CHUNK, :], gc_p[pair - 1:pair, :])
                e_tail = jnp.exp(g_end - gc_p)
                beta_p = beta_s[rows, :]
            g_col = gc_p[:, la:la + 1]
            g_row = gct_s[la:la + 1, rows]
            b_col = beta_p[:, h:h + 1]
            eg_col = e_gc[:, la:la + 1]
            qh = q_s[rows, head_cols(h)]
            kh = k_s[rows, head_cols(h)]
            vh = v_s[rows, head_cols(h)]
            g_diff = jnp.where(left, g_col[:CHUNK], g_col[CHUNK:]) - g_row
            decay = jnp.where(causal, jnp.exp(jnp.where(causal, g_diff, 0.0)), 0.0)
            kb = kh * b_col
            kq = _mm_nt(jnp.concatenate([kb, qh], axis=0), kh)
            lows.append(jnp.where(strict, side_by_side(kq[:pair]) * decay, 0.0))
            a_s[rows, head_cols(h)] = block_diag((side_by_side(kq[pair:]) * decay).astype(BF16))
            rhs.append(jnp.concatenate([vh * b_col, kb * eg_col], axis=1).astype(BF16))
            qd_s[rows, head_cols(h)] = (qh * eg_col).astype(BF16)
            kt_s[rows, head_cols(h)] = (kh * e_tail[:, la:la + 1]).astype(BF16)
            yield
        invs = [eye - jnp.where(xor_ij == 1, low, 0.0) for low in lows]
        level = 1
        while (1 << level) < CHUNK:
            joins = (xor_ij >> level) == 1
            cs = [block_diag(jnp.where(joins, low, 0.0).astype(BF16)) for low in lows]
            xs = [inv.astype(BF16) for inv in invs]
            xcs = []
            for x, c in zip(xs, cs):
                xcs.append(_mm(x, c))
                yield
            for i, (xc, x) in enumerate(zip(xcs, xs)):
                invs[i] = invs[i] - _mm(xc, block_diag(x))
                yield
            level += 1
        for (r0, h), t_inv, r in zip(problems, invs, rhs):
            rows = slice(r0, r0 + pair)
            uw = _mm(block_diag(t_inv.astype(BF16)), r)
            u_s[rows, head_cols(h)] = uw[:, :DN_HEAD_DIM]
            w_s[rows, head_cols(h)] = uw[:, DN_HEAD_DIM:].astype(BF16)
            yield

    levels = CHUNK.bit_length() - 2
    factor_steps = (2 + 2 * levels) * pairs_per_sub * DN_HEADS

    def recur(j):
        for p in range(pairs_per_sub):
            p0 = j * sub + p * pair
            v_new = [[None, None] for _ in heads]
            q_state = [[None, None] for _ in heads]
            for cc in range(2):
                r0 = p0 + cc * CHUNK
                rows = slice(r0, r0 + CHUNK)
                e_end = jnp.exp(gc_s[r0 + CHUNK - 1:r0 + CHUNK, :])
                s_in = [state[h] for h in heads]
                prods = []
                for h in heads:
                    lhs = jnp.concatenate([w_s[rows, head_cols(h)], qd_s[rows, head_cols(h)]], axis=0)
                    prods.append(_mm(lhs, s_in[h]))
                for h in heads:
                    v_new[h][cc] = u_s[rows, head_cols(h)] - prods[h][:CHUNK]
                    q_state[h][cc] = prods[h][CHUNK:]
                yield
                for h in heads:
                    la = DN_HEADS + h
                    state[h] = (s_in[h] * e_end[:, la:la + 1]
                                + _mm_tn(kt_s[rows, head_cols(h)], v_new[h][cc]))
                yield
            rows = slice(p0, p0 + pair)
            for h in heads:
                o = (jnp.concatenate(q_state[h], axis=0)
                     + _mm(a_s[rows, head_cols(h)], jnp.concatenate(v_new[h], axis=0)))
                o = o * lax.rsqrt(jnp.mean(o * o, axis=-1, keepdims=True) + NORM_EPS) * normw_ref[...]
                y_ref[rows, head_cols(h)] = (o * _silu(z_s[rows, head_cols(h)])).astype(y_ref.dtype)
                yield

    recur_steps = (4 + DN_HEADS) * pairs_per_sub

    stages = ((front, front_steps), (factor, factor_steps), (recur, recur_steps))
    for slot in range(n_sub + len(stages) - 1):
        _interleave([(stage(slot - k), steps) for k, (stage, steps) in enumerate(stages)
                     if 0 <= slot - k < n_sub])


def _w_all_spec(name, rows):
    off, width = _W_ALL_LAYOUT[name]
    return pl.BlockSpec((rows, width), lambda *_: (0, off // width))


def _deltanet_branch(x, w_all, conv_w, alog_row, dtb_row, normw_row):
    b, s, d = x.shape
    tile = DN_TILE
    assert d == D_MODEL and s % tile == 0 and tile % DN_SUB == 0 and DN_SUB % (2 * CHUNK) == 0
    assert (3 * DN_WIDTH) % DN_PROJ_COLS == 0 and DN_PROJ_COLS % DN_HEAD_DIM == 0
    const = lambda bi, ti: (0, 0)
    return pl.pallas_call(
        _dn_kernel,
        grid=(b, s // tile),
        in_specs=[
            pl.BlockSpec((None, tile, d), lambda bi, ti: (bi, ti, 0)),
            _w_all_spec("dn_qkv", d),
            _w_all_spec("dn_z", d),
            _w_all_spec("dn_ba", d),
            pl.BlockSpec(conv_w.shape, const),
            pl.BlockSpec(alog_row.shape, const),
            pl.BlockSpec(dtb_row.shape, const),
            pl.BlockSpec(normw_row.shape, const),
        ],
        out_specs=pl.BlockSpec((None, tile, DN_WIDTH), lambda bi, ti: (bi, ti, 0)),
        out_shape=jax.ShapeDtypeStruct((b, s, DN_WIDTH), BF16),
        scratch_shapes=[
            pltpu.VMEM((tile, d), BF16),
            pltpu.VMEM((tile + SUBLANES, 3 * DN_WIDTH), F32),
            pltpu.VMEM((tile, DN_WIDTH), F32),
            pltpu.VMEM((tile, DN_WIDTH), F32),
            pltpu.VMEM((tile, DN_WIDTH), F32),
            pltpu.VMEM((tile, DN_WIDTH), F32),
            pltpu.VMEM((tile, LANES), F32),
            pltpu.VMEM((tile, LANES), F32),
            pltpu.VMEM((LANES, tile), F32),
            pltpu.VMEM((tile, DN_WIDTH), F32),
            pltpu.VMEM((tile, DN_WIDTH), BF16),
            pltpu.VMEM((tile, DN_WIDTH), BF16),
            pltpu.VMEM((tile, DN_WIDTH), BF16),
            pltpu.VMEM((tile, DN_WIDTH), BF16),
            pltpu.VMEM((DN_HEADS, DN_HEAD_DIM, DN_HEAD_DIM), F32),
        ],
        compiler_params=pltpu.CompilerParams(
            dimension_semantics=("arbitrary", "arbitrary"), vmem_limit_bytes=VMEM_LIMIT_BYTES),
        name="deltanet_branch",
    )(x, w_all, w_all, w_all, conv_w, alog_row, dtb_row, normw_row)


def _swa_kernel(sinks_ref, x_ref, wq_ref, wkv_ref, wz_ref, rope_off_ref, rope_start_ref, y_ref,
                kband, vband, q_s, z_s):
    t = pl.program_id(1)
    w = WINDOW
    tile = x_ref.shape[0]

    @pl.when(t == 0)
    def _():
        kband[:, 0:w, :] = jnp.zeros((2 * SWA_KV_HEADS, w, LANES), BF16)
        vband[:, 0:w, :] = jnp.zeros((2 * SWA_KV_HEADS, w, LANES), BF16)

    half = ROPE_DIM // 2
    sub = SWA_SUB
    n_sub = tile // sub
    n_pairs = SWA_Q_HEADS // 2
    group = SWA_Q_HEADS // SWA_KV_HEADS
    lane = lax.broadcasted_iota(jnp.int32, (sub, LANES), 1)
    lo = lane < SWA_HEAD_DIM
    head_lane = lane % SWA_HEAD_DIM

    qi = lax.broadcasted_iota(jnp.int32, (w, 2 * w), 0)
    kj = lax.broadcasted_iota(jnp.int32, (w, 2 * w), 1)
    in_band = (kj > qi) & (kj <= qi + w)
    bias = jnp.where(in_band, 0.0, MASK_VALUE).astype(F32)
    bias_first = jnp.where(in_band & (kj >= jnp.where(t == 0, w, 0)), 0.0, MASK_VALUE).astype(F32)

    def front(j):
        r0 = j * sub
        rows = slice(r0, r0 + sub)
        cos_o = rope_off_ref[rows, 0:LANES]
        sin_o = rope_off_ref[rows, LANES:2 * LANES]
        cos_s = rope_start_ref[:, 0:LANES]
        sin_s = rope_start_ref[:, LANES:2 * LANES]
        cos_p = cos_s * cos_o - sin_s * sin_o
        sin_p = sin_s * cos_o + cos_s * sin_o
        sin_a = jnp.where(head_lane < half, -sin_p, 0.0)
        sin_b = jnp.where(head_lane >= half, sin_p, 0.0)

        def rope(v):
            return v * cos_p + pltpu.roll(v, LANES - half, 1) * sin_a + pltpu.roll(v, half, 1) * sin_b

        xb = x_ref[rows, :].astype(BF16)
        q = jnp.dot(xb, wq_ref[...], preferred_element_type=F32)
        yield
        for pair in range(n_pairs):
            p0 = pair * LANES
            q_s[rows, p0:p0 + LANES] = rope(q[:, p0:p0 + LANES]).astype(BF16)
        yield
        kv = jnp.dot(xb, wkv_ref[...], preferred_element_type=F32)
        z_s[rows, :] = jnp.dot(xb, wz_ref[...], preferred_element_type=F32)
        yield
        k = rope(kv[:, :LANES])
        v = kv[:, LANES:]
        band_rows = slice(w + r0, w + r0 + sub)
        for src, band in ((k, kband), (v, vband)):
            swapped = pltpu.roll(src, SWA_HEAD_DIM, 1)
            band[0, band_rows, :] = jnp.where(lo, src, 0.0).astype(BF16)
            band[1, band_rows, :] = jnp.where(lo, 0.0, swapped).astype(BF16)
            band[2, band_rows, :] = jnp.where(lo, swapped, 0.0).astype(BF16)
            band[3, band_rows, :] = jnp.where(lo, 0.0, src).astype(BF16)
        yield

    front_steps = 4

    slabs_per_group = group // 2
    upper_rows = lax.broadcasted_iota(jnp.int32, (slabs_per_group * w, 1), 0) >= w

    def kv_group(r0, g, blk_bias):
        slabs = [slice((slabs_per_group * g + i) * LANES, (slabs_per_group * g + i + 1) * LANES)
                 for i in range(slabs_per_group)]
        q_rows = jnp.concatenate([q_s[r0:r0 + w, cols] for cols in slabs], axis=0)
        bias_rows = jnp.concatenate([blk_bias] * slabs_per_group, axis=0)
        scores, sinks = [], []
        for hf in range(2):
            heads_hf = [group * g + 2 * i + hf for i in range(slabs_per_group)]
            sinks.append(jnp.where(upper_rows, sinks_ref[heads_hf[1]], sinks_ref[heads_hf[0]]))
            scores.append(_mm_nt(q_rows, kband[2 * g + hf, r0:r0 + 2 * w, :]) + bias_rows)
        yield
        probs, denoms = [], []
        for s, sink in zip(scores, sinks):
            m = jnp.maximum(jnp.max(jnp.maximum(s[:, :w], s[:, w:]), axis=-1, keepdims=True), sink)
            p = jnp.exp(s - m)
            denoms.append(jnp.sum(p[:, :w] + p[:, w:], axis=-1, keepdims=True) + jnp.exp(sink - m))
            probs.append(p.astype(BF16))
        yield
        acc = None
        for hf in range(2):
            o = _mm(probs[hf], vband[2 * g + hf, r0:r0 + 2 * w, :]) / denoms[hf]
            acc = o if acc is None else acc + o
        for i, cols in enumerate(slabs):
            y_ref[r0:r0 + w, cols] = (
                acc[i * w:(i + 1) * w] * _silu(z_s[r0:r0 + w, cols])).astype(y_ref.dtype)
        yield

    def attend(j):
        items = [(j * sub + b * w, g) for b in range(sub // w) for g in range(SWA_KV_HEADS)]
        for g0 in range(0, len(items), SWA_GROUPS_IN_FLIGHT):
            streams = [kv_group(r0, g, bias_first if r0 == 0 else bias)
                       for r0, g in items[g0:g0 + SWA_GROUPS_IN_FLIGHT]]
            for _ in range(3):
                for stream in streams:
                    next(stream)
                yield

    attend_steps = 3 * (sub // w) * SWA_KV_HEADS // SWA_GROUPS_IN_FLIGHT

    stages = ((front, front_steps), (attend, attend_steps))
    for slot in range(n_sub + len(stages) - 1):
        _interleave([(stage(slot - k), steps) for k, (stage, steps) in enumerate(stages)
                     if 0 <= slot - k < n_sub])

    kband[:, 0:w, :] = kband[:, tile:tile + w, :]
    vband[:, 0:w, :] = vband[:, tile:tile + w, :]


def _swa_branch(x, sinks, w_all):
    b, s, d = x.shape
    w = WINDOW
    tile = SWA_TILE
    assert d == D_MODEL and s % tile == 0 and tile % SWA_SUB == 0 and SWA_SUB % w == 0
    assert SWA_Q_HEADS // SWA_KV_HEADS == 4 and 2 * SWA_HEAD_DIM == LANES
    assert (SWA_SUB // w * SWA_KV_HEADS) % SWA_GROUPS_IN_FLIGHT == 0
    rope_offsets, rope_starts = _rope_tables(s, tile)
    return pl.pallas_call(
        _swa_kernel,
        grid=(b, s // tile),
        in_specs=[
            pl.BlockSpec(memory_space=pltpu.SMEM),
            pl.BlockSpec((None, tile, d), lambda bi, ti: (bi, ti, 0)),
            _w_all_spec("swa_q", d),
            _w_all_spec("swa_kv", d),
            _w_all_spec("swa_z", d),
            pl.BlockSpec((tile, 2 * LANES), lambda bi, ti: (0, 0)),
            pl.BlockSpec((None, 1, 2 * LANES), lambda bi, ti: (ti, 0, 0)),
        ],
        out_specs=pl.BlockSpec((None, tile, SWA_WIDTH), lambda bi, ti: (bi, ti, 0)),
        out_shape=jax.ShapeDtypeStruct((b, s, SWA_WIDTH), BF16),
        scratch_shapes=[
            pltpu.VMEM((2 * SWA_KV_HEADS, w + tile, LANES), BF16),
            pltpu.VMEM((2 * SWA_KV_HEADS, w + tile, LANES), BF16),
            pltpu.VMEM((tile, SWA_WIDTH), BF16),
            pltpu.VMEM((tile, SWA_WIDTH), F32),
        ],
        compiler_params=pltpu.CompilerParams(
            dimension_semantics=("arbitrary", "arbitrary"), vmem_limit_bytes=VMEM_LIMIT_BYTES),
        name="swa_branch",
    )(sinks, x, w_all, w_all, w_all, rope_offsets, rope_starts)


def _rope_tables(seq, tile):
    inv_freq = ROPE_THETA ** (-jnp.arange(0, ROPE_DIM, 2, dtype=F32) / ROPE_DIM)

    def lane_pattern(positions):
        ang = positions[:, None] * inv_freq[None, :]
        n = positions.shape[0]
        rest = SWA_HEAD_DIM - ROPE_DIM
        reps = LANES // SWA_HEAD_DIM
        cos = jnp.concatenate([jnp.cos(ang)] * 2 + [jnp.ones((n, rest), F32)], axis=1)
        sin = jnp.concatenate([jnp.sin(ang)] * 2 + [jnp.zeros((n, rest), F32)], axis=1)
        return jnp.concatenate([jnp.tile(cos, (1, reps)), jnp.tile(sin, (1, reps))], axis=1)

    offsets = lane_pattern(jnp.arange(tile, dtype=F32))
    starts = lane_pattern(jnp.arange(seq // tile, dtype=F32) * tile)
    return offsets, starts.reshape(seq // tile, 1, 2 * LANES)


def _merge_kernel(alpha, x_ref, ya_ref, yb_ref, wg_ref, wa_ref, wb_ref, wo_ref, lng_ref, lnb_ref, o_ref,
                  merged_s):
    tile = x_ref.shape[0]
    sub = MERGE_SUB
    n_sub = tile // sub

    def gated_merge(j):
        rows = slice(j * sub, (j + 1) * sub)
        xb = x_ref[rows, :].astype(BF16)
        gate_a = jax.nn.sigmoid(jnp.dot(xb, wg_ref[:, :D_MODEL], preferred_element_type=F32))
        pa = jnp.dot(ya_ref[rows, :], wa_ref[...], preferred_element_type=F32)
        yield
        gate_b = jax.nn.sigmoid(jnp.dot(xb, wg_ref[:, D_MODEL:], preferred_element_type=F32))
        pb = jnp.dot(yb_ref[rows, :], wb_ref[...], preferred_element_type=F32)
        yield
        merged_s[rows, :] = (gate_a * pa + gate_b * pb).astype(BF16)
        yield

    def project_norm(j):
        for r0 in range(j * sub, (j + 1) * sub, sub // 2):
            rows = slice(r0, r0 + sub // 2)
            out = jnp.dot(merged_s[rows, :], wo_ref[...], preferred_element_type=F32)
            yield
            r = alpha * x_ref[rows, :] + out
            mu = jnp.mean(r, axis=-1, keepdims=True)
            cen = r - mu
            var = jnp.mean(cen * cen, axis=-1, keepdims=True)
            o_ref[rows, :] = (cen * lax.rsqrt(var + LN_EPS) * lng_ref[...] + lnb_ref[...]).astype(o_ref.dtype)
            yield

    stages = ((gated_merge, 3), (project_norm, 4))
    for slot in range(n_sub + len(stages) - 1):
        _interleave([(stage(slot - k), steps) for k, (stage, steps) in enumerate(stages)
                     if 0 <= slot - k < n_sub])


def _merge(x2, ya2, yb2, w_all, wa, wb, wo, lng_row, lnb_row, alpha):
    n, d = x2.shape
    tile = MERGE_TILE
    assert d == D_MODEL and n % tile == 0 and tile % MERGE_SUB == 0
    const = lambda i: (0, 0)
    row = lambda i: (i, 0)
    return pl.pallas_call(
        functools.partial(_merge_kernel, alpha),
        grid=(n // tile,),
        in_specs=[
            pl.BlockSpec((tile, d), row),
            pl.BlockSpec((tile, ya2.shape[1]), row),
            pl.BlockSpec((tile, yb2.shape[1]), row),
            _w_all_spec("gates", d),
            pl.BlockSpec(wa.shape, const),
            pl.BlockSpec(wb.shape, const),
            pl.BlockSpec(wo.shape, const),
            pl.BlockSpec(lng_row.shape, const),
            pl.BlockSpec(lnb_row.shape, const),
        ],
        out_specs=pl.BlockSpec((tile, d), row),
        out_shape=jax.ShapeDtypeStruct((n, d), x2.dtype),
        scratch_shapes=[pltpu.VMEM((tile, d), BF16)],
        compiler_params=pltpu.CompilerParams(
            dimension_semantics=("arbitrary",), vmem_limit_bytes=VMEM_LIMIT_BYTES),
        name="merge_out_norm",
    )(x2, ya2, yb2, w_all, wa, wb, wo, lng_row, lnb_row)


def _lane_row(vec, offset):
    n = vec.shape[0]
    return jnp.pad(vec.astype(F32), (offset, LANES - offset - n)).reshape(1, LANES)


def _pack_w_in(w_in):
    o = _OFF_SWA
    groups = {
        "dn_qkv": w_in[:, _OFF_QKV:_OFF_DN_Z],
        "dn_z": w_in[:, _OFF_DN_Z:_OFF_DN_BA],
        "gates": w_in[:, _OFF_GATE:],
        "swa_q": w_in[:, o:o + SWA_WIDTH] * (SWA_HEAD_DIM ** -0.5),
        "swa_z": w_in[:, o + SWA_WIDTH + 2 * SWA_KV_WIDTH:_OFF_GATE],
        "swa_kv": w_in[:, o + SWA_WIDTH:o + SWA_WIDTH + 2 * SWA_KV_WIDTH],
        "dn_ba": jnp.pad(w_in[:, _OFF_DN_BA:_OFF_SWA], ((0, 0), (0, LANES - 2 * DN_HEADS))),
    }
    assert list(groups) == list(_W_ALL_LAYOUT)
    return jnp.concatenate([g.astype(BF16) for g in groups.values()], axis=1)


def _layer(x, w_in, conv_w, a_log, dt_bias, dn_norm_w, sinks, w_branch, w_out, ln_g, ln_b, alpha):
    b, s, d = x.shape
    w_all = _pack_w_in(w_in)
    y_a = _deltanet_branch(x, w_all, conv_w.astype(F32), _lane_row(a_log, DN_HEADS),
                           _lane_row(dt_bias, DN_HEADS), dn_norm_w.astype(F32).reshape(1, DN_HEAD_DIM))
    y_b = _swa_branch(x, sinks.astype(F32), w_all)
    w_ab = w_branch.astype(BF16)
    out = _merge(x.reshape(b * s, d), y_a.reshape(b * s, DN_WIDTH), y_b.reshape(b * s, SWA_WIDTH),
                 w_all, w_ab[0], w_ab[1], w_out.astype(BF16),
                 ln_g.astype(F32).reshape(1, d), ln_b.astype(F32).reshape(1, d), alpha)
    return out.reshape(b, s, d)


def kernel(x, w_in, conv_w, a_log, dt_bias, dn_norm_w, sinks, w_branch, w_out, ln_g, ln_b):
    depth = w_in.shape[0]
    assert w_in.shape[1:] == (D_MODEL, _OFF_GATE + 2 * D_MODEL) and conv_w.shape[1:] == (CONV_WIDTH, 3 * DN_WIDTH)
    alpha = (2.0 * depth) ** 0.25
    for layer in range(depth):
        x = _layer(x, w_in[layer], conv_w[layer], a_log[layer], dt_bias[layer], dn_norm_w[layer],
                   sinks[layer], w_branch[layer], w_out[layer], ln_g[layer], ln_b[layer], alpha)
    return x
```
